```python
import jax, jax.numpy as jnp
from jax import lax
import numpy as np

D_MODEL = 1024
BATCH = 8
SEQ = 4096
DEPTH = 4

GRID_W = 64
CTX_LEN = 256
N_MIXERS = 2
N_HEADS = 16
HEAD_DIM = D_MODEL // N_HEADS
N_DIRS = 2
DECAY_LORA = 64
AAA_LORA = 64
MV_LORA = 32
GATE_LORA = 128
FOURIER_GROUPS = 4
MLP_HIDDEN = 4 * D_MODEL
N_MOD = 6
RMS_EPS = 1e-6
LNX_EPS = 64e-5
L2_EPS = 1e-24

kernel_name = 'hybrid_rwkv7_fnet_adaln_block'


def _rmsnorm(x, g):
    x32 = x.astype(jnp.float32)
    y = x32 * lax.rsqrt(jnp.mean(x32 * x32, axis=-1, keepdims=True) + RMS_EPS)
    return (y * g.astype(jnp.float32)).astype(x.dtype)


def _modulate(h, shift, scale):
    return h * (1 + scale) + shift


def _mlp(h, w1, w2):
    return jnp.square(jax.nn.relu(h @ w1)) @ w2


def _heads(t):
    return t.reshape(t.shape[0], t.shape[1], N_HEADS, HEAD_DIM)


def _qshift(h):
    b, t, d = h.shape
    rows = t // GRID_W
    g = h.reshape(b, rows, GRID_W, 4, d // 4)
    left = jnp.pad(g[:, :, :-1, 0], ((0, 0), (0, 0), (1, 0), (0, 0)))
    right = jnp.pad(g[:, :, 1:, 1], ((0, 0), (0, 0), (0, 1), (0, 0)))
    up = jnp.pad(g[:, :-1, :, 2], ((0, 0), (1, 0), (0, 0), (0, 0)))
    down = jnp.pad(g[:, 1:, :, 3], ((0, 0), (0, 1), (0, 0), (0, 0)))
    return jnp.stack([left, right, up, down], axis=3).reshape(b, t, d)


def _shift_ctx(h):
    d = h.shape[-1]
    prev = jnp.pad(h[:, :-1, : d // 2], ((0, 0), (1, 0), (0, 0)))
    nxt = jnp.pad(h[:, 1:, d // 2:], ((0, 0), (0, 1), (0, 0)))
    return jnp.concatenate([prev, nxt], axis=-1)


def _wkv_scan(r, v, w, k, a, b, s0, reverse):
    dt = v.dtype
    with_out = r is not None
    ins = (w, k, v, a, b) + ((r,) if with_out else ())
    xs = tuple(jnp.swapaxes(t.astype(jnp.float32), 0, 1) for t in ins)

    def step(s, inp):
        w_t, k_t, v_t, a_t, b_t = inp[:5]
        sa = jnp.einsum('bhvk,bhk->bhv', s, a_t)
        s = s * w_t[:, :, None, :] + sa[..., None] * b_t[:, :, None, :] + v_t[..., None] * k_t[:, :, None, :]
        y = jnp.einsum('bhvk,bhk->bhv', s, inp[5]) if with_out else None
        return s, y

    s_fin, ys = lax.scan(step, s0, xs, reverse=reverse)
    if with_out:
        ys = jnp.swapaxes(ys, 0, 1).astype(dt)
    return ys, s_fin


def _rwkv_streams(h, shifted, lp, vres, v_first, readout):
    xx = shifted - h
    mu = lp['mu']
    xr, xw, xk, xv, xa, xg = (h + xx * mu[j] for j in range(6))
    k = xk @ lp['wk']
    v = xv @ lp['wv']
    if vres is not None:
        v0, v1, v2 = vres
        v = v + (v_first - v) * jax.nn.sigmoid(v0 + (xv @ v1) @ v2)
    kk = _heads(k * lp['kk']).astype(jnp.float32)
    kk = kk * lax.rsqrt(jnp.maximum(jnp.sum(kk * kk, axis=-1, keepdims=True), L2_EPS))
    dirs = []
    for d in range(N_DIRS):
        w_log = -jax.nn.softplus(-(lp['w0'][d] + jnp.tanh(xw @ lp['w1'][d]) @ lp['w2'][d])) - 0.5
        decay = jnp.exp(-jnp.exp(w_log.astype(jnp.float32)))
        iclr = jax.nn.sigmoid(lp['a0'][d] + (xa @ lp['a1'][d]) @ lp['a2'][d])
        k_d = k * (1 + (iclr - 1) * lp['ka'])
        dirs.append((_heads(decay), _heads(k_d), -kk, kk * _heads(iclr).astype(jnp.float32)))
    st = {'v': v, 'dirs': dirs}
    if readout:
        st['r'] = _heads(xr @ lp['wr'])
        st['g'] = jax.nn.sigmoid(xg @ lp['g1']) @ lp['g2']
    return st


def _rwkv_readout(y, st, lp):
    b, l = y.shape[:2]
    y32 = y.astype(jnp.float32)
    mean = jnp.mean(y32, axis=-1, keepdims=True)
    var = jnp.mean(jnp.square(y32 - mean), axis=-1, keepdims=True)
    yn = ((y32 - mean) * lax.rsqrt(var + LNX_EPS)).reshape(b, l, D_MODEL) * lp['lnx_w'] + lp['lnx_b']
    r = st['r']
    vh = _heads(st['v'])
    bonus = sum(jnp.sum(r * kd * lp['rk'], axis=-1, keepdims=True) for (_, kd, _, _) in st['dirs']) * vh
    out = (yn.astype(st['v'].dtype) + bonus.reshape(b, l, D_MODEL)) * st['g']
    return out @ lp['wo']


def _rwkv_mixer(h_ctx, h_lat, lp, vres, vf_ctx, vf_lat, ctx_readout):
    st_c = _rwkv_streams(h_ctx, _shift_ctx(h_ctx), lp, vres, vf_ctx, ctx_readout)
    st_l = _rwkv_streams(h_lat, _qshift(h_lat), lp, vres, vf_lat, True)
    vh_c = _heads(st_c['v'])
    vh_l = _heads(st_l['v'])
    s0 = jnp.zeros((h_lat.shape[0], N_HEADS, HEAD_DIM, HEAD_DIM), jnp.float32)
    y_ctx = []
    y_lat = []
    for d in range(N_DIRS):
        rev = d == 1
        yc, s_ctx = _wkv_scan(st_c.get('r'), vh_c, *st_c['dirs'][d], s0, rev)
        yl, _ = _wkv_scan(st_l['r'], vh_l, *st_l['dirs'][d], s_ctx, rev)
        y_lat.append(yl)
        y_ctx.append(yc)
    o_lat = _rwkv_readout(y_lat[0] + y_lat[1], st_l, lp)
    o_ctx = _rwkv_readout(y_ctx[0] + y_ctx[1], st_c, lp) if ctx_readout else None
    return o_ctx, o_lat, st_c['v'], st_l['v']


def _fourier_mixer(h, wo):
    b, l, d = h.shape
    g = h.astype(jnp.float32).reshape(b, l, FOURIER_GROUPS, d // FOURIER_GROUPS)
    f = jnp.fft.fftn(g, axes=(1, 3), norm='ortho').real
    return f.reshape(b, l, d).astype(h.dtype) @ wo


def setup_inputs(seed: int = 0) -> dict:
    key = jax.random.key(seed)
    ks = iter(jax.random.split(key, 40))
    D = D_MODEL
    F = MLP_HIDDEN
    n_a = (DEPTH + 1) // 2
    n_b = DEPTH // 2
    n_v = n_a - 1

    def nrm(shape, s):
        return jax.random.normal(next(ks), shape, jnp.float32) * s

    def unif(shape, lo, hi):
        return jax.random.uniform(next(ks), shape, jnp.float32, minval=lo, maxval=hi)

    return {
        'x': nrm((BATCH, SEQ, D), 1.0),
        'c': nrm((BATCH, D), 1.0),
        'ctx': nrm((BATCH, CTX_LEN, D), 1.0),
        'c_ctx': nrm((D,), 1.0),
        'norm1_g': 1.0 + nrm((DEPTH, D), 0.02),
        'norm2_g': 1.0 + nrm((DEPTH, D), 0.02),
        'mod_w': nrm((DEPTH, D, N_MOD * D), 0.5 * D ** -0.5),
        'mod_b': nrm((DEPTH, N_MOD * D), 0.02),
        'mlp_w1': nrm((DEPTH, D, F), D ** -0.5),
        'mlp_w2': nrm((DEPTH, F, D), F ** -0.5),
        'rk_mu': unif((n_a, 6, D), 0.0, 1.0),
        'rk_wr': nrm((n_a, D, D), D ** -0.5),
        'rk_wk': nrm((n_a, D, D), D ** -0.5),
        'rk_wv': nrm((n_a, D, D), D ** -0.5),
        'rk_wo': nrm((n_a, D, D), D ** -0.5),
        'rk_w0': unif((n_a, N_DIRS, D), -3.0, 0.5),
        'rk_w1': nrm((n_a, N_DIRS, D, DECAY_LORA), D ** -0.5),
        'rk_w2': nrm((n_a, N_DIRS, DECAY_LORA, D), 0.5 * DECAY_LORA ** -0.5),
        'rk_a0': nrm((n_a, N_DIRS, D), 0.1),
        'rk_a1': nrm((n_a, N_DIRS, D, AAA_LORA), D ** -0.5),
        'rk_a2': nrm((n_a, N_DIRS, AAA_LORA, D), 0.5 * AAA_LORA ** -0.5),
        'rk_v0': 0.5 + nrm((n_v, D), 0.1),
        'rk_v1': nrm((n_v, D, MV_LORA), D ** -0.5),
        'rk_v2': nrm((n_v, MV_LORA, D), 0.5 * MV_LORA ** -0.5),
        'rk_g1': nrm((n_a, D, GATE_LORA), D ** -0.5),
        'rk_g2': nrm((n_a, GATE_LORA, D), GATE_LORA ** -0.5),
        'rk_kk': 0.85 + nrm((n_a, D), 0.02),
        'rk_ka': 1.0 + nrm((n_a, D), 0.02),
        'rk_rk': nrm((n_a, N_HEADS, HEAD_DIM), 0.1),
        'rk_lnx_w': 1.0 + nrm((n_a, D), 0.02),
        'rk_lnx_b': nrm((n_a, D), 0.02),
        'ft_wo': nrm((n_b, D, D), D ** -0.5),
        'final_g': 1.0 + nrm((D,), 0.02),
    }


def reference(x, c, ctx, c_ctx, norm1_g, norm2_g, mod_w, mod_b, mlp_w1, mlp_w2, rk_mu, rk_wr, rk_wk, rk_wv, rk_wo, rk_w0, rk_w1, rk_w2, rk_a0, rk_a1, rk_a2, rk_v0, rk_v1, rk_v2, rk_g1, rk_g2, rk_kk, rk_ka, rk_rk, rk_lnx_w, rk_lnx_b, ft_wo, final_g):
    last_a = ((DEPTH - 1) // N_MIXERS) * N_MIXERS
    silu_c = jax.nn.silu(c)
    silu_cc = jax.nn.silu(c_ctx)
    vf_ctx = None
    vf_lat = None
    for i in range(DEPTH):
        mixer = i % N_MIXERS
        idx = i // N_MIXERS
        ctx_in = i <= last_a
        ctx_out = i < last_a
        sh1, sc1, ga1, sh2, sc2, ga2 = jnp.split((silu_c @ mod_w[i] + mod_b[i])[:, None, :], N_MOD, axis=-1)
        h_lat = _modulate(_rmsnorm(x, norm1_g[i]), sh1, sc1)
        if ctx_in:
            csh1, csc1, cga1, csh2, csc2, cga2 = jnp.split(silu_cc @ mod_w[i] + mod_b[i], N_MOD, axis=-1)
            h_ctx = _modulate(_rmsnorm(ctx, norm1_g[i]), csh1, csc1)
        if mixer == 0:
            lp = {'mu': rk_mu[idx], 'wr': rk_wr[idx], 'wk': rk_wk[idx], 'wv': rk_wv[idx], 'wo': rk_wo[idx],
                  'w0': rk_w0[idx], 'w1': rk_w1[idx], 'w2': rk_w2[idx],
                  'a0': rk_a0[idx], 'a1': rk_a1[idx], 'a2': rk_a2[idx],
                  'g1': rk_g1[idx], 'g2': rk_g2[idx], 'kk': rk_kk[idx], 'ka': rk_ka[idx], 'rk': rk_rk[idx],
                  'lnx_w': rk_lnx_w[idx], 'lnx_b': rk_lnx_b[idx]}
            vres = (rk_v0[idx - 1], rk_v1[idx - 1], rk_v2[idx - 1]) if idx > 0 else None
            o_ctx, o_lat, v_c, v_l = _rwkv_mixer(h_ctx, h_lat, lp, vres, vf_ctx, vf_lat, ctx_out)
            if idx == 0:
                vf_ctx, vf_lat = v_c, v_l
        else:
            o_lat = _fourier_mixer(h_lat, ft_wo[idx])
            o_ctx = _fourier_mixer(h_ctx, ft_wo[idx]) if ctx_out else None
        x = x + ga1 * o_lat
        x = x + ga2 * _mlp(_modulate(_rmsnorm(x, norm2_g[i]), sh2, sc2), mlp_w1[i], mlp_w2[i])
        if ctx_out:
            ctx = ctx + cga1 * o_ctx
            ctx = ctx + cga2 * _mlp(_modulate(_rmsnorm(ctx, norm2_g[i]), csh2, csc2), mlp_w1[i], mlp_w2[i])
    return _rmsnorm(x, final_g)
```

```python
import functools

import jax
import jax.numpy as jnp
from jax import lax
from jax.experimental import pallas as pl
from jax.experimental.pallas import tpu as pltpu

HEAD_DIM = 64
GRID_W = 64
N_MOD = 6
FOURIER_GROUPS = 4
RMS_EPS = 1e-6
LNX_EPS = 64e-5
L2_EPS = 1e-24

MOD_ROWS = 16
LANES = 128
WKV_TC = 32
VMEM_LIMIT = 56 * 1024 * 1024

F32 = jnp.float32
BF16 = jnp.bfloat16


def _cparams(sem):
    return pltpu.CompilerParams(dimension_semantics=sem, vmem_limit_bytes=VMEM_LIMIT)


def _const_spec(shape):
    nd = len(shape)
    return pl.BlockSpec(shape, lambda *_: (0,) * nd, pipeline_mode=pl.Buffered(1))


def _dot(a, b):
    return jnp.dot(a, b, preferred_element_type=F32)


def _split2(x):
    hi = x.astype(BF16)
    lo = (x - hi.astype(F32)).astype(BF16)
    return hi, lo


def _head_sum(x, s1, s2):
    hi, lo = _split2(x)
    s = _dot(hi, s1) + _dot(lo, s1)
    h1 = s.astype(BF16)
    r1 = s - h1.astype(F32)
    h2 = r1.astype(BF16)
    h3 = (r1 - h2.astype(F32)).astype(BF16)
    return _dot(h1, s2) + _dot(h2, s2) + _dot(h3, s2)


def _norm_mod(x, g, sh, sc):
    ms = jnp.mean(x * x, axis=-1, keepdims=True)
    return (x * lax.rsqrt(ms + RMS_EPS)) * g * (1.0 + sc) + sh


def _mod_kernel(c_ref, w_ref, b_ref, o_ref):
    c = c_ref[...]
    s = c * jax.nn.sigmoid(c)
    o_ref[...] = jnp.dot(s, w_ref[...], precision=lax.Precision.HIGHEST,
                         preferred_element_type=F32) + b_ref[...]


def _mod_table(cvec, mod_w, mod_b):
    depth, d, nd = mod_w.shape
    nt = nd // d
    return pl.pallas_call(
        _mod_kernel,
        grid=(depth, nt),
        in_specs=[
            pl.BlockSpec((MOD_ROWS, d), lambda l, j: (0, 0)),
            pl.BlockSpec((None, d, d), lambda l, j: (l, 0, j)),
            pl.BlockSpec((None, 1, d), lambda l, j: (l, 0, j)),
        ],
        out_specs=pl.BlockSpec((None, MOD_ROWS, d), lambda l, j: (l, 0, j)),
        out_shape=jax.ShapeDtypeStruct((depth, MOD_ROWS, nd), F32),
        compiler_params=_cparams(("arbitrary", "arbitrary")),
        name="mod_table",
    )(cvec, mod_w, mod_b.reshape(depth, 1, nd))


class _Stream:
    def __init__(self, b, l, d, ctx_row=None):
        self.b, self.l, self.d, self.ctx_row = b, l, d, ctx_row

    def grid(self, tm):
        assert self.l % tm == 0
        return (self.b, self.l // tm)

    def tok_spec(self, tm, width=None):
        return pl.BlockSpec((None, tm, width or self.d), lambda b, i: (b, i, 0))

    def row_spec(self, tm):
        return pl.BlockSpec((None, tm, self.d // LANES, LANES), lambda b, i: (b, i, 0, 0))

    def dir_row_spec(self, tm):
        return pl.BlockSpec((2, None, tm, self.d // LANES, LANES), lambda b, i: (0, b, i, 0, 0))

    def row_shape(self):
        return jax.ShapeDtypeStruct((self.b, self.l, self.d // LANES, LANES), F32)

    def dir_row_shape(self):
        return jax.ShapeDtypeStruct((2, self.b, self.l, self.d // LANES, LANES), F32)

    def mod_row(self, b):
        return b if self.ctx_row is None else b * 0 + self.ctx_row

    def mod_spec(self):
        return pl.BlockSpec((None, N_MOD, self.d), lambda b, *_: (self.mod_row(b), 0, 0))

    def shape(self, width=None, dtype=F32):
        return jax.ShapeDtypeStruct((self.b, self.l, width or self.d), dtype)


_PAR2 = ("parallel", "parallel")


def _prenorm_kernel(x_ref, g_ref, m_ref, o_ref):
    m = m_ref[...]
    o_ref[...] = _norm_mod(x_ref[...], g_ref[...], m[0:1], m[1:2])


def _prenorm(st, x, g, mods, tm=256):
    return pl.pallas_call(
        _prenorm_kernel,
        grid=st.grid(tm),
        in_specs=[st.tok_spec(tm), _const_spec((1, st.d)), st.mod_spec()],
        out_specs=st.tok_spec(tm),
        out_shape=st.shape(),
        compiler_params=_cparams(_PAR2),
        name="prenorm",
    )(x, g.reshape(1, st.d), mods)


def _qshift(h):
    b, t, d = h.shape
    rows = t // GRID_W
    g = h.reshape(b, rows, GRID_W, 4, d // 4)
    left = jnp.pad(g[:, :, :-1, 0], ((0, 0), (0, 0), (1, 0), (0, 0)))
    right = jnp.pad(g[:, :, 1:, 1], ((0, 0), (0, 0), (0, 1), (0, 0)))
    up = jnp.pad(g[:, :-1, :, 2], ((0, 0), (1, 0), (0, 0), (0, 0)))
    down = jnp.pad(g[:, 1:, :, 3], ((0, 0), (0, 1), (0, 0), (0, 0)))
    return jnp.stack([left, right, up, down], axis=3).reshape(b, t, d)


def _shift_ctx(h):
    d = h.shape[-1]
    prev = jnp.pad(h[:, :-1, : d // 2], ((0, 0), (1, 0), (0, 0)))
    nxt = jnp.pad(h[:, 1:, d // 2:], ((0, 0), (0, 1), (0, 0)))
    return jnp.concatenate([prev, nxt], axis=-1)


def _rwkv_proj_kernel(has_vres, *refs):
    if has_vres:
        (h_ref, s_ref, vf_ref, mu_ref, wr_ref, wk_ref, wv_ref, w1_ref, w2_ref, w0_ref,
         a1_ref, a2_ref, a0_ref, g1_ref, g2_ref, kk_ref, ka_ref, rk_ref, s1_ref, s2_ref,
         v0_ref, v1_ref, v2_ref,
         v_o, g_o, bonus_o, ykr_o, vr_o, a_o, w_o, k_o, b_o, q_o) = refs
    else:
        (h_ref, s_ref, mu_ref, wr_ref, wk_ref, wv_ref, w1_ref, w2_ref, w0_ref,
         a1_ref, a2_ref, a0_ref, g1_ref, g2_ref, kk_ref, ka_ref, rk_ref, s1_ref, s2_ref,
         v_o, g_o, bonus_o, ykr_o, vr_o, a_o, w_o, k_o, b_o, q_o) = refs
    tm, d = h_ref.shape
    rows = lambda t: t.reshape(tm, d // LANES, LANES)
    h = h_ref[...]
    xx = s_ref[...] - h
    mu = mu_ref[...]
    s1 = s1_ref[...]
    s2 = s2_ref[...]

    def mix(j):
        return (h + xx * mu[j:j + 1]).astype(BF16)

    r = _dot(mix(0), wr_ref[...])
    k = _dot(mix(2), wk_ref[...])
    xv = mix(3)
    v = _dot(xv, wv_ref[...])
    if has_vres:
        lo = _dot(_dot(xv, v1_ref[...]).astype(BF16), v2_ref[...])
        v = v + (vf_ref[...] - v) * jax.nn.sigmoid(v0_ref[...] + lo)
    kk = k * kk_ref[...]
    kk = kk * lax.rsqrt(jnp.maximum(_head_sum(kk * kk, s1, s2), L2_EPS))
    wl = jnp.tanh(_dot(mix(1), w1_ref[...])).astype(BF16)
    wd = w0_ref[...] + _dot(wl, w2_ref[...])
    al = _dot(mix(4), a1_ref[...]).astype(BF16)
    iclr2 = jax.nn.sigmoid(a0_ref[...] + _dot(al, a2_ref[...]))
    g = _dot(jax.nn.sigmoid(_dot(mix(5), g1_ref[...])).astype(BF16), g2_ref[...])
    ka = ka_ref[...]
    rk = rk_ref[...]
    a = -kk
    v_o[...] = v
    g_o[...] = g
    vr_o[...] = rows(v)
    a_o[...] = rows(a)
    bonus = jnp.zeros_like(r)
    ykr = jnp.zeros_like(r)
    for dr in range(2):
        z = -wd[:, dr * d:(dr + 1) * d]
        softplus = jnp.maximum(z, 0.0) + jnp.log1p(jnp.exp(-jnp.abs(z)))
        decay = jnp.exp(-jnp.exp(-softplus - 0.5))
        iclr = iclr2[:, dr * d:(dr + 1) * d]
        kd = k * (1.0 + (iclr - 1.0) * ka)
        bd = kk * iclr
        br = _head_sum(bd * r, s1, s2)
        kr = _head_sum(kd * r, s1, s2)
        w_o[dr] = rows(decay)
        k_o[dr] = rows(kd)
        b_o[dr] = rows(bd)
        q_o[dr] = rows(decay * r + a * br)
        ykr = ykr + v * kr
        bonus = bonus + r * kd * rk
    bonus_o[...] = _head_sum(bonus, s1, s2) * v
    ykr_o[...] = ykr


def _rwkv_proj(st, h, shifted, vfirst, p, tm=256):
    has_vres = vfirst is not None
    tok = st.tok_spec(tm)
    ins = [h, shifted] + ([vfirst] if has_vres else [])
    specs = [tok, tok] + ([tok] if has_vres else [])
    consts = [p['mu'], p['wr'], p['wk'], p['wv'], p['w1'], p['w2'], p['w0'],
              p['a1'], p['a2'], p['a0'], p['g1'], p['g2'], p['kk'], p['ka'], p['rk'],
              p['s1'], p['s2']]
    if has_vres:
        consts += [p['v0'], p['v1'], p['v2']]
    ins += consts
    specs += [_const_spec(c.shape) for c in consts]
    return pl.pallas_call(
        functools.partial(_rwkv_proj_kernel, has_vres),
        grid=st.grid(tm),
        in_specs=specs,
        out_specs=[tok] * 4 + [st.row_spec(tm)] * 2 + [st.dir_row_spec(tm)] * 4,
        out_shape=[st.shape()] * 4 + [st.row_shape()] * 2 + [st.dir_row_shape()] * 4,
        compiler_params=_cparams(_PAR2),
        name="rwkv_proj",
    )(*ins)


def _wkv_kernel(nb, tc, a_ref, v_ref, w_ref, b_ref, k_ref, q_ref, s0_ref, j_ref, e_ref,
                y_ref, sf_ref, s_ref):
    dr = pl.program_id(0)
    ch = pl.program_id(1)
    npair = a_ref.shape[2]

    @pl.when(ch == 0)
    def _():
        s_ref[...] = s0_ref[...]

    jm = j_ref[...]
    em = e_ref[...]
    em_all = jnp.concatenate([em] * npair, axis=0)

    def bcast(ref, bi, t):
        return jnp.concatenate(
            [jnp.broadcast_to(ref[bi, t, pl.ds(p, 1), :], (HEAD_DIM, LANES))
             for p in range(npair)], axis=0)

    def step(i, carry):
        t = jnp.where(dr == 1, tc - 1 - i, i)
        for bi in range(nb):
            s = s_ref[bi]
            a_b = bcast(a_ref, bi, t)
            v_b = bcast(v_ref, bi, t)
            w_b = bcast(w_ref, bi, t)
            b_b = bcast(b_ref, bi, t)
            k_b = bcast(k_ref, bi, t)
            q_b = bcast(q_ref, bi, t)
            pq = jnp.concatenate([s * a_b, s * q_b], axis=1).astype(BF16)
            red = _dot(pq, jm)
            vh, vl = _split2(em_all * v_b)
            vcol = _dot(jnp.concatenate([vh, vl], axis=1), jm)
            vcol = vcol[:, :LANES] + vcol[:, LANES:]
            sa = red[:, :LANES]
            yq = red[:, LANES:] * em_all
            s_ref[bi] = s * w_b + sa * b_b + vcol * k_b
            for p in range(npair):
                y_ref[bi, t, pl.ds(p, 1), :] = jnp.sum(
                    yq[p * HEAD_DIM:(p + 1) * HEAD_DIM], axis=0, keepdims=True)
        return carry

    lax.fori_loop(0, tc, step, 0)

    @pl.when(ch == pl.num_programs(1) - 1)
    def _():
        sf_ref[...] = s_ref[...]


def _wkv(st, a, v, w, b, k, q, s0):
    d, nb, l = st.d, st.b, st.l
    npair = d // LANES
    tc = min(WKV_TC, l)
    nch = l // tc

    def chunk(dr, j):
        return jnp.where(dr == 1, nch - 1 - j, j)

    sh_spec = pl.BlockSpec((nb, tc, npair, LANES), lambda dr, j: (0, chunk(dr, j), 0, 0))
    dir_spec = pl.BlockSpec((None, nb, tc, npair, LANES), lambda dr, j: (dr, 0, chunk(dr, j), 0, 0))
    st_spec = pl.BlockSpec((None, nb, npair * HEAD_DIM, LANES), lambda dr, j: (dr, 0, 0, 0))
    eye = (jnp.arange(HEAD_DIM)[:, None] == (jnp.arange(LANES)[None, :] % HEAD_DIM)).astype(F32)
    blk = (jnp.arange(2 * LANES)[:, None] // HEAD_DIM == jnp.arange(2 * LANES)[None, :] // HEAD_DIM)
    jmat = blk.astype(BF16)
    return pl.pallas_call(
        functools.partial(_wkv_kernel, nb, tc),
        grid=(2, nch),
        in_specs=[sh_spec, sh_spec, dir_spec, dir_spec, dir_spec, dir_spec, st_spec,
                  _const_spec(jmat.shape), _const_spec(eye.shape)],
        out_specs=[dir_spec, st_spec],
        out_shape=[st.dir_row_shape(), jax.ShapeDtypeStruct(s0.shape, F32)],
        scratch_shapes=[pltpu.VMEM((nb, npair * HEAD_DIM, LANES), F32)],
        compiler_params=_cparams(("arbitrary", "arbitrary")),
        name="wkv",
    )(a, v, w, b, k, q, s0, jmat, eye)


def _rwkv_out_kernel(x_ref, y_ref, ykr_ref, bonus_ref, g_ref, m_ref, lw_ref, lb_ref, wo_ref,
                     s1_ref, s2_ref, o_ref):
    s1 = s1_ref[...]
    s2 = s2_ref[...]
    y = (y_ref[0] + y_ref[1]).reshape(x_ref.shape) + ykr_ref[...]
    mean = _head_sum(y, s1, s2) * (1.0 / HEAD_DIM)
    yc = y - mean
    var = _head_sum(yc * yc, s1, s2) * (1.0 / HEAD_DIM)
    yn = (yc * lax.rsqrt(var + LNX_EPS)) * lw_ref[...] + lb_ref[...]
    out = (yn + bonus_ref[...]) * g_ref[...]
    o = _dot(out.astype(BF16), wo_ref[...])
    o_ref[...] = x_ref[...] + m_ref[...][2:3] * o


def _rwkv_out(st, x, y, ykr, bonus, g, mods, p, tm=256):
    tok = st.tok_spec(tm)
    consts = [p['lnx_w'], p['lnx_b'], p['wo'], p['s1'], p['s2']]
    return pl.pallas_call(
        _rwkv_out_kernel,
        grid=st.grid(tm),
        in_specs=[tok, st.dir_row_spec(tm), tok, tok, tok, st.mod_spec()]
        + [_const_spec(c.shape) for c in consts],
        out_specs=tok,
        out_shape=st.shape(),
        compiler_params=_cparams(_PAR2),
        name="rwkv_out",
    )(x, y, ykr, bonus, g, mods, *consts)


def _mlp_kernel(x_ref, g_ref, m_ref, w1_ref, w2_ref, o_ref):
    x = x_ref[...]
    m = m_ref[...]
    hn = _norm_mod(x, g_ref[...], m[3:4], m[4:5]).astype(BF16)
    d = x.shape[-1]
    f = w1_ref.shape[-1]
    acc = jnp.zeros_like(x)
    for c in range(f // d):
        hid = jnp.maximum(_dot(hn, w1_ref[:, c * d:(c + 1) * d]), 0.0)
        acc = acc + _dot((hid * hid).astype(BF16), w2_ref[c * d:(c + 1) * d, :])
    o_ref[...] = x + m[5:6] * acc


def _mlp(st, x, g, mods, w1, w2, tm=256):
    tm = min(tm, st.l)
    return pl.pallas_call(
        _mlp_kernel,
        grid=st.grid(tm),
        in_specs=[st.tok_spec(tm), _const_spec((1, st.d)), st.mod_spec(),
                  _const_spec(w1.shape), _const_spec(w2.shape)],
        out_specs=st.tok_spec(tm),
        out_shape=st.shape(),
        compiler_params=_cparams(_PAR2),
        name="mlp",
    )(x, g.reshape(1, st.d), mods, w1, w2)


def _dft_mats(n):
    idx = jnp.arange(n, dtype=jnp.int32)
    ang = ((idx[:, None] * idx[None, :]) % n).astype(F32) * (2.0 * jnp.pi / n)
    return jnp.cos(ang), jnp.sin(ang)


def _fnet_chan_kernel(x_ref, g_ref, m_ref, c_ref, s_ref, o_ref):
    m = m_ref[...]
    d = x_ref.shape[-1]
    cg = d // FOURIER_GROUPS
    h = _norm_mod(x_ref[...], g_ref[...], m[0:1], m[1:2]).astype(BF16)
    for gi in range(FOURIER_GROUPS):
        hg = h[:, gi * cg:(gi + 1) * cg]
        o_ref[:, gi * cg:(gi + 1) * cg] = _dot(hg, c_ref[...]).astype(BF16)
        o_ref[:, d + gi * cg:d + (gi + 1) * cg] = _dot(hg, s_ref[...]).astype(BF16)


def _fnet_chan(st, x, g, mods, cc, sc, tm=256):
    return pl.pallas_call(
        _fnet_chan_kernel,
        grid=st.grid(tm),
        in_specs=[st.tok_spec(tm), _const_spec((1, st.d)), st.mod_spec(),
                  _const_spec(cc.shape), _const_spec(sc.shape)],
        out_specs=st.tok_spec(tm, 2 * st.d),
        out_shape=st.shape(2 * st.d, BF16),
        compiler_params=_cparams(_PAR2),
        name="fnet_chan",
    )(x, g.reshape(1, st.d), mods, cc, sc)


def _fnet_seq_kernel(scale, x_ref, xcs_ref, cl_ref, sl_ref, m_ref, wo_ref, o_ref, acc_ref):
    kstep = pl.program_id(2)
    d = x_ref.shape[-1]

    @pl.when(kstep == 0)
    def _():
        acc_ref[...] = jnp.zeros_like(acc_ref)

    acc_ref[...] += _dot(cl_ref[...], xcs_ref[:, :d]) - _dot(sl_ref[...], xcs_ref[:, d:])

    @pl.when(kstep == pl.num_programs(2) - 1)
    def _():
        f = (acc_ref[...] * scale).astype(BF16)
        o_ref[...] = x_ref[...] + m_ref[...][2:3] * _dot(f, wo_ref[...])


def _fnet_seq(st, x, xcs, mods, wo, tl=1024, tk=512):
    d, l = st.d, st.l
    tl = min(tl, l)
    tk = min(tk, l)
    cl, sl = _dft_mats(l)
    cl = cl.astype(BF16)
    sl = sl.astype(BF16)
    scale = 1.0 / float(l * (d // FOURIER_GROUPS)) ** 0.5
    return pl.pallas_call(
        functools.partial(_fnet_seq_kernel, scale),
        grid=(st.b, l // tl, l // tk),
        in_specs=[
            pl.BlockSpec((None, tl, d), lambda b, i, k: (b, i, 0)),
            pl.BlockSpec((None, tk, 2 * d), lambda b, i, k: (b, k, 0)),
            pl.BlockSpec((tl, tk), lambda b, i, k: (i, k)),
            pl.BlockSpec((tl, tk), lambda b, i, k: (i, k)),
            st.mod_spec(),
            _const_spec(wo.shape),
        ],
        out_specs=pl.BlockSpec((None, tl, d), lambda b, i, k: (b, i, 0)),
        out_shape=st.shape(),
        scratch_shapes=[pltpu.VMEM((tl, d), F32)],
        compiler_params=_cparams(("parallel", "parallel", "arbitrary")),
        name="fnet_seq",
    )(x, xcs, cl, sl, mods, wo)


def _final_kernel(x_ref, g_ref, o_ref):
    x = x_ref[...]
    ms = jnp.mean(x * x, axis=-1, keepdims=True)
    o_ref[...] = (x * lax.rsqrt(ms + RMS_EPS)) * g_ref[...]


def _final_norm(st, x, g, tm=256):
    return pl.pallas_call(
        _final_kernel,
        grid=st.grid(tm),
        in_specs=[st.tok_spec(tm), _const_spec((1, st.d))],
        out_specs=st.tok_spec(tm),
        out_shape=st.shape(),
        compiler_params=_cparams(_PAR2),
        name="final_norm",
    )(x, g.reshape(1, st.d))


def _blockdiag2(w):
    z = jnp.zeros_like(w[0])
    return jnp.concatenate([jnp.concatenate([w[0], z], axis=1),
                            jnp.concatenate([z, w[1]], axis=1)], axis=0)


def kernel(x, c, ctx, c_ctx, norm1_g, norm2_g, mod_w, mod_b, mlp_w1, mlp_w2, rk_mu, rk_wr, rk_wk, rk_wv, rk_wo, rk_w0, rk_w1, rk_w2, rk_a0, rk_a1, rk_a2, rk_v0, rk_v1, rk_v2, rk_g1, rk_g2, rk_kk, rk_ka, rk_rk, rk_lnx_w, rk_lnx_b, ft_wo, final_g):
    nb, seq, d = x.shape
    ctx_len = ctx.shape[1]
    depth = mod_w.shape[0]
    assert nb + 1 <= MOD_ROWS and d % LANES == 0
    lat = _Stream(nb, seq, d)
    cst = _Stream(nb, ctx_len, d, ctx_row=nb)
    row = lambda t: t.reshape(1, -1)

    cvec = jnp.zeros((MOD_ROWS, d), F32).at[:nb].set(c).at[nb].set(c_ctx)
    mods_all = _mod_table(cvec, mod_w, mod_b).reshape(depth, MOD_ROWS, N_MOD, d)

    heads = jnp.arange(d) // HEAD_DIM
    s1 = (heads[:, None] == jnp.arange(LANES)[None, :]).astype(BF16)
    s2 = s1.T
    cg = d // FOURIER_GROUPS
    cc, sc = _dft_mats(cg)
    cc = cc.astype(BF16)
    sc = sc.astype(BF16)
    zero_state = jnp.zeros((2, nb, d // LANES * HEAD_DIM, LANES), F32)

    last_a = ((depth - 1) // 2) * 2
    vf_lat = vf_ctx = None
    for i in range(depth):
        idx = i // 2
        mods = mods_all[i]
        ctx_in = i <= last_a
        ctx_out = i < last_a
        if i % 2 == 0:
            p = {
                'mu': rk_mu[idx], 'wr': rk_wr[idx].astype(BF16), 'wk': rk_wk[idx].astype(BF16),
                'wv': rk_wv[idx].astype(BF16), 'wo': rk_wo[idx].astype(BF16),
                'w1': jnp.concatenate([rk_w1[idx, 0], rk_w1[idx, 1]], axis=1).astype(BF16),
                'w2': _blockdiag2(rk_w2[idx]).astype(BF16), 'w0': row(rk_w0[idx]),
                'a1': jnp.concatenate([rk_a1[idx, 0], rk_a1[idx, 1]], axis=1).astype(BF16),
                'a2': _blockdiag2(rk_a2[idx]).astype(BF16), 'a0': row(rk_a0[idx]),
                'g1': rk_g1[idx].astype(BF16), 'g2': rk_g2[idx].astype(BF16),
                'kk': row(rk_kk[idx]), 'ka': row(rk_ka[idx]), 'rk': row(rk_rk[idx]),
                'lnx_w': row(rk_lnx_w[idx]), 'lnx_b': row(rk_lnx_b[idx]), 's1': s1, 's2': s2,
            }
            if idx > 0:
                p['v0'] = row(rk_v0[idx - 1])
                p['v1'] = rk_v1[idx - 1].astype(BF16)
                p['v2'] = rk_v2[idx - 1].astype(BF16)
            h_c = _prenorm(cst, ctx, norm1_g[i], mods)
            v_c, g_c, bonus_c, ykr_c, vr_c, a_c, w_c, k_c, b_c, q_c = _rwkv_proj(
                cst, h_c, _shift_ctx(h_c), vf_ctx if idx > 0 else None, p)
            y_c, s_ctx = _wkv(cst, a_c, vr_c, w_c, b_c, k_c, q_c, zero_state)
            h_l = _prenorm(lat, x, norm1_g[i], mods)
            v_l, g_l, bonus_l, ykr_l, vr_l, a_l, w_l, k_l, b_l, q_l = _rwkv_proj(
                lat, h_l, _qshift(h_l), vf_lat if idx > 0 else None, p)
            y_l, _ = _wkv(lat, a_l, vr_l, w_l, b_l, k_l, q_l, s_ctx)
            if idx == 0:
                vf_ctx, vf_lat = v_c, v_l
            x = _rwkv_out(lat, x, y_l, ykr_l, bonus_l, g_l, mods, p)
            if ctx_out:
                ctx = _rwkv_out(cst, ctx, y_c, ykr_c, bonus_c, g_c, mods, p)
        else:
            wo = ft_wo[idx].astype(BF16)
            x = _fnet_seq(lat, x, _fnet_chan(lat, x, norm1_g[i], mods, cc, sc), mods, wo)
            if ctx_out:
                ctx = _fnet_seq(cst, ctx, _fnet_chan(cst, ctx, norm1_g[i], mods, cc, sc), mods, wo)
        w1 = mlp_w1[i].astype(BF16)
        w2 = mlp_w2[i].astype(BF16)
        x = _mlp(lat, x, norm2_g[i], mods, w1, w2, tm=512)
        if ctx_out:
            ctx = _mlp(cst, ctx, norm2_g[i], mods, w1, w2)
    return _final_norm(lat, x, final_g)
```

```python
import functools

import jax
import jax.numpy as jnp
from jax import lax
from jax.experimental import pallas as pl
from jax.experimental.pallas import tpu as pltpu

HEAD_DIM = 64
GRID_W = 64
N_MOD = 6
FOURIER_GROUPS = 4
RMS_EPS = 1e-6
LNX_EPS = 64e-5
L2_EPS = 1e-24

MOD_ROWS = 16
LANES = 128
WKV_TC = HEAD_DIM
VMEM_LIMIT = 56 * 1024 * 1024

F32 = jnp.float32
BF16 = jnp.bfloat16


def _cparams(sem):
    return pltpu.CompilerParams(dimension_semantics=sem, vmem_limit_bytes=VMEM_LIMIT)


def _const_spec(shape):
    nd = len(shape)
    return pl.BlockSpec(shape, lambda *_: (0,) * nd, pipeline_mode=pl.Buffered(1))


def _dot(a, b):
    return jnp.dot(a, b, preferred_element_type=F32)


def _split2(x):
    hi = x.astype(BF16)
    lo = (x - hi.astype(F32)).astype(BF16)
    return hi, lo


def _head_sum(x, s1, s2):
    hi, lo = _split2(x)
    s = _dot(hi, s1) + _dot(lo, s1)
    h1 = s.astype(BF16)
    r1 = s - h1.astype(F32)
    h2 = r1.astype(BF16)
    h3 = (r1 - h2.astype(F32)).astype(BF16)
    return _dot(h1, s2) + _dot(h2, s2) + _dot(h3, s2)


def _norm_mod(x, g, sh, sc):
    ms = jnp.mean(x * x, axis=-1, keepdims=True)
    return (x * lax.rsqrt(ms + RMS_EPS)) * g * (1.0 + sc) + sh


def _mod_kernel(c_ref, w_ref, b_ref, o_ref):
    c = c_ref[...]
    s = c * jax.nn.sigmoid(c)
    o_ref[...] = jnp.dot(s, w_ref[...], precision=lax.Precision.HIGHEST,
                         preferred_element_type=F32) + b_ref[...]


def _mod_table(cvec, mod_w, mod_b):
    depth, d, nd = mod_w.shape
    nt = nd // d
    return pl.pallas_call(
        _mod_kernel,
        grid=(depth, nt),
        in_specs=[
            pl.BlockSpec((MOD_ROWS, d), lambda l, j: (0, 0)),
            pl.BlockSpec((None, d, d), lambda l, j: (l, 0, j)),
            pl.BlockSpec((None, 1, d), lambda l, j: (l, 0, j)),
        ],
        out_specs=pl.BlockSpec((None, MOD_ROWS, d), lambda l, j: (l, 0, j)),
        out_shape=jax.ShapeDtypeStruct((depth, MOD_ROWS, nd), F32),
        compiler_params=_cparams(("arbitrary", "arbitrary")),
        name="mod_table",
    )(cvec, mod_w, mod_b.reshape(depth, 1, nd))


class _Stream:
    def __init__(self, b, l, d, ctx_row=None):
        self.b, self.l, self.d, self.ctx_row = b, l, d, ctx_row

    def grid(self, tm):
        assert self.l % tm == 0
        return (self.b, self.l // tm)

    def tok_spec(self, tm, width=None):
        return pl.BlockSpec((None, tm, width or self.d), lambda b, i: (b, i, 0))

    def row_spec(self, tm):
        return pl.BlockSpec((None, tm, self.d // LANES, LANES), lambda b, i: (b, i, 0, 0))

    def dir_row_spec(self, tm):
        return pl.BlockSpec((2, None, tm, self.d // LANES, LANES), lambda b, i: (0, b, i, 0, 0))

    def row_shape(self):
        return jax.ShapeDtypeStruct((self.b, self.l, self.d // LANES, LANES), F32)

    def dir_row_shape(self):
        return jax.ShapeDtypeStruct((2, self.b, self.l, self.d // LANES, LANES), F32)

    def mod_row(self, b):
        return b if self.ctx_row is None else b * 0 + self.ctx_row

    def mod_spec(self):
        return pl.BlockSpec((None, N_MOD, self.d), lambda b, *_: (self.mod_row(b), 0, 0))

    def shape(self, width=None, dtype=F32):
        return jax.ShapeDtypeStruct((self.b, self.l, width or self.d), dtype)


_PAR2 = ("parallel", "parallel")


def _prenorm_kernel(x_ref, g_ref, m_ref, o_ref):
    m = m_ref[...]
    o_ref[...] = _norm_mod(x_ref[...], g_ref[...], m[0:1], m[1:2])


def _prenorm(st, x, g, mods, tm=256):
    return pl.pallas_call(
        _prenorm_kernel,
        grid=st.grid(tm),
        in_specs=[st.tok_spec(tm), _const_spec((1, st.d)), st.mod_spec()],
        out_specs=st.tok_spec(tm),
        out_shape=st.shape(),
        compiler_params=_cparams(_PAR2),
        name="prenorm",
    )(x, g.reshape(1, st.d), mods)


def _qshift(h):
    b, t, d = h.shape
    rows = t // GRID_W
    g = h.reshape(b, rows, GRID_W, 4, d // 4)
    left = jnp.pad(g[:, :, :-1, 0], ((0, 0), (0, 0), (1, 0), (0, 0)))
    right = jnp.pad(g[:, :, 1:, 1], ((0, 0), (0, 0), (0, 1), (0, 0)))
    up = jnp.pad(g[:, :-1, :, 2], ((0, 0), (1, 0), (0, 0), (0, 0)))
    down = jnp.pad(g[:, 1:, :, 3], ((0, 0), (0, 1), (0, 0), (0, 0)))
    return jnp.stack([left, right, up, down], axis=3).reshape(b, t, d)


def _shift_ctx(h):
    d = h.shape[-1]
    prev = jnp.pad(h[:, :-1, : d // 2], ((0, 0), (1, 0), (0, 0)))
    nxt = jnp.pad(h[:, 1:, d // 2:], ((0, 0), (0, 1), (0, 0)))
    return jnp.concatenate([prev, nxt], axis=-1)


def _rwkv_proj_kernel(has_vres, *refs):
    if has_vres:
        (h_ref, s_ref, vf_ref, mu_ref, wr_ref, wk_ref, wv_ref, w1_ref, w2_ref, w0_ref,
         a1_ref, a2_ref, a0_ref, g1_ref, g2_ref, kk_ref, ka_ref, rk_ref, s1_ref, s2_ref,
         v0_ref, v1_ref, v2_ref,
         v_o, g_o, bonus_o, ykr_o, vr_o, a_o, w_o, k_o, b_o, q_o) = refs
    else:
        (h_ref, s_ref, mu_ref, wr_ref, wk_ref, wv_ref, w1_ref, w2_ref, w0_ref,
         a1_ref, a2_ref, a0_ref, g1_ref, g2_ref, kk_ref, ka_ref, rk_ref, s1_ref, s2_ref,
         v_o, g_o, bonus_o, ykr_o, vr_o, a_o, w_o, k_o, b_o, q_o) = refs
    tm, d = h_ref.shape
    rows = lambda t: t.reshape(tm, d // LANES, LANES)
    h = h_ref[...]
    xx = s_ref[...] - h
    mu = mu_ref[...]
    s1 = s1_ref[...]
    s2 = s2_ref[...]

    def mix(j):
        return (h + xx * mu[j:j + 1]).astype(BF16)

    r = _dot(mix(0), wr_ref[...])
    k = _dot(mix(2), wk_ref[...])
    xv = mix(3)
    v = _dot(xv, wv_ref[...])
    if has_vres:
        lo = _dot(_dot(xv, v1_ref[...]).astype(BF16), v2_ref[...])
        v = v + (vf_ref[...] - v) * jax.nn.sigmoid(v0_ref[...] + lo)
    kk = k * kk_ref[...]
    kk = kk * lax.rsqrt(jnp.maximum(_head_sum(kk * kk, s1, s2), L2_EPS))
    wl = jnp.tanh(_dot(mix(1), w1_ref[...])).astype(BF16)
    wd = w0_ref[...] + _dot(wl, w2_ref[...])
    al = _dot(mix(4), a1_ref[...]).astype(BF16)
    iclr2 = jax.nn.sigmoid(a0_ref[...] + _dot(al, a2_ref[...]))
    g = _dot(jax.nn.sigmoid(_dot(mix(5), g1_ref[...])).astype(BF16), g2_ref[...])
    ka = ka_ref[...]
    rk = rk_ref[...]
    a = -kk
    v_o[...] = v
    g_o[...] = g
    vr_o[...] = rows(v)
    a_o[...] = rows(a)
    bonus = jnp.zeros_like(r)
    ykr = jnp.zeros_like(r)
    for dr in range(2):
        z = -wd[:, dr * d:(dr + 1) * d]
        softplus = jnp.maximum(z, 0.0) + jnp.log1p(jnp.exp(-jnp.abs(z)))
        decay = jnp.exp(-jnp.exp(-softplus - 0.5))
        iclr = iclr2[:, dr * d:(dr + 1) * d]
        kd = k * (1.0 + (iclr - 1.0) * ka)
        bd = kk * iclr
        br = _head_sum(bd * r, s1, s2)
        kr = _head_sum(kd * r, s1, s2)
        w_o[dr] = rows(decay)
        k_o[dr] = rows(kd)
        b_o[dr] = rows(bd)
        q_o[dr] = rows(decay * r + a * br)
        ykr = ykr + v * kr
        bonus = bonus + r * kd * rk
    bonus_o[...] = _head_sum(bonus, s1, s2) * v
    ykr_o[...] = ykr


def _rwkv_proj(st, h, shifted, vfirst, p, tm=256):
    has_vres = vfirst is not None
    tok = st.tok_spec(tm)
    ins = [h, shifted] + ([vfirst] if has_vres else [])
    specs = [tok, tok] + ([tok] if has_vres else [])
    consts = [p['mu'], p['wr'], p['wk'], p['wv'], p['w1'], p['w2'], p['w0'],
              p['a1'], p['a2'], p['a0'], p['g1'], p['g2'], p['kk'], p['ka'], p['rk'],
              p['s1'], p['s2']]
    if has_vres:
        consts += [p['v0'], p['v1'], p['v2']]
    ins += consts
    specs += [_const_spec(c.shape) for c in consts]
    return pl.pallas_call(
        functools.partial(_rwkv_proj_kernel, has_vres),
        grid=st.grid(tm),
        in_specs=specs,
        out_specs=[tok] * 4 + [st.row_spec(tm)] * 2 + [st.dir_row_spec(tm)] * 4,
        out_shape=[st.shape()] * 4 + [st.row_shape()] * 2 + [st.dir_row_shape()] * 4,
        compiler_params=_cparams(_PAR2),
        name="rwkv_proj",
    )(*ins)


def _wkv_kernel(nb, tc, a_ref, v_ref, w_ref, b_ref, k_ref, q_ref, s0_ref, j_ref, e_ref,
                y_ref, sf_ref, s_ref, vt_ref):
    dr = pl.program_id(0)
    ch = pl.program_id(1)
    npair = a_ref.shape[2]
    rows = npair * HEAD_DIM

    @pl.when(ch == 0)
    def _():
        s_ref[...] = s0_ref[...]

    for bi in range(nb):
        for p in range(npair):
            vt = v_ref[bi, :, p, :].T
            vt_ref[bi, p * HEAD_DIM:(p + 1) * HEAD_DIM, :] = jnp.concatenate(
                [vt[:HEAD_DIM], vt[HEAD_DIM:]], axis=1)

    jm = j_ref[...]
    em = e_ref[...]
    em_all = jnp.concatenate([em] * npair, axis=0)
    lane_head = (lax.broadcasted_iota(jnp.int32, (rows, LANES), 1) // HEAD_DIM) * HEAD_DIM

    def bcast(ref, bi, t):
        return jnp.concatenate(
            [jnp.broadcast_to(ref[bi, t, pl.ds(p, 1), :], (HEAD_DIM, LANES))
             for p in range(npair)], axis=0)

    def step(i, carry):
        t = jnp.where(dr == 1, tc - 1 - i, i)
        idx = lane_head + t
        for bi in range(nb):
            s = s_ref[bi]
            a_b = bcast(a_ref, bi, t)
            w_b = bcast(w_ref, bi, t)
            b_b = bcast(b_ref, bi, t)
            k_b = bcast(k_ref, bi, t)
            q_b = bcast(q_ref, bi, t)
            vcol = jnp.take_along_axis(vt_ref[bi], idx, axis=1)
            red = _dot(jnp.concatenate([s * a_b, s * q_b], axis=1), jm)
            sa = red[:, :LANES]
            yq = red[:, LANES:] * em_all
            s_ref[bi] = s * w_b + sa * b_b + vcol * k_b
            for p in range(npair):
                y_ref[bi, t, pl.ds(p, 1), :] = jnp.sum(
                    yq[p * HEAD_DIM:(p + 1) * HEAD_DIM], axis=0, keepdims=True)
        return carry

    lax.fori_loop(0, tc, step, 0)

    @pl.when(ch == pl.num_programs(1) - 1)
    def _():
        sf_ref[...] = s_ref[...]


def _wkv(st, a, v, w, b, k, q, s0):
    d, nb, l = st.d, st.b, st.l
    npair = d // LANES
    tc = WKV_TC
    assert l % tc == 0
    nch = l // tc

    def chunk(dr, j):
        return jnp.where(dr == 1, nch - 1 - j, j)

    sh_spec = pl.BlockSpec((nb, tc, npair, LANES), lambda dr, j: (0, chunk(dr, j), 0, 0))
    dir_spec = pl.BlockSpec((None, nb, tc, npair, LANES), lambda dr, j: (dr, 0, chunk(dr, j), 0, 0))
    st_spec = pl.BlockSpec((None, nb, npair * HEAD_DIM, LANES), lambda dr, j: (dr, 0, 0, 0))
    eye = (jnp.arange(HEAD_DIM)[:, None] == (jnp.arange(LANES)[None, :] % HEAD_DIM)).astype(F32)
    blk = (jnp.arange(2 * LANES)[:, None] // HEAD_DIM == jnp.arange(2 * LANES)[None, :] // HEAD_DIM)
    jmat = blk.astype(F32)
    return pl.pallas_call(
        functools.partial(_wkv_kernel, nb, tc),
        grid=(2, nch),
        in_specs=[sh_spec, sh_spec, dir_spec, dir_spec, dir_spec, dir_spec, st_spec,
                  _const_spec(jmat.shape), _const_spec(eye.shape)],
        out_specs=[dir_spec, st_spec],
        out_shape=[st.dir_row_shape(), jax.ShapeDtypeStruct(s0.shape, F32)],
        scratch_shapes=[pltpu.VMEM((nb, npair * HEAD_DIM, LANES), F32),
                        pltpu.VMEM((nb, npair * HEAD_DIM, LANES), F32)],
        compiler_params=_cparams(("arbitrary", "arbitrary")),
        name="wkv",
    )(a, v, w, b, k, q, s0, jmat, eye)


def _rwkv_out_kernel(x_ref, y_ref, ykr_ref, bonus_ref, g_ref, m_ref, lw_ref, lb_ref, wo_ref,
                     s1_ref, s2_ref, o_ref):
    s1 = s1_ref[...]
    s2 = s2_ref[...]
    y = (y_ref[0] + y_ref[1]).reshape(x_ref.shape) + ykr_ref[...]
    mean = _head_sum(y, s1, s2) * (1.0 / HEAD_DIM)
    yc = y - mean
    var = _head_sum(yc * yc, s1, s2) * (1.0 / HEAD_DIM)
    yn = (yc * lax.rsqrt(var + LNX_EPS)) * lw_ref[...] + lb_ref[...]
    out = (yn + bonus_ref[...]) * g_ref[...]
    o = _dot(out.astype(BF16), wo_ref[...])
    o_ref[...] = x_ref[...] + m_ref[...][2:3] * o


def _rwkv_out(st, x, y, ykr, bonus, g, mods, p, tm=256):
    tok = st.tok_spec(tm)
    consts = [p['lnx_w'], p['lnx_b'], p['wo'], p['s1'], p['s2']]
    return pl.pallas_call(
        _rwkv_out_kernel,
        grid=st.grid(tm),
        in_specs=[tok, st.dir_row_spec(tm), tok, tok, tok, st.mod_spec()]
        + [_const_spec(c.shape) for c in consts],
        out_specs=tok,
        out_shape=st.shape(),
        compiler_params=_cparams(_PAR2),
        name="rwkv_out",
    )(x, y, ykr, bonus, g, mods, *consts)


def _mlp_kernel(x_ref, g_ref, m_ref, w1_ref, w2_ref, o_ref):
    x = x_ref[...]
    m = m_ref[...]
    hn = _norm_mod(x, g_ref[...], m[3:4], m[4:5]).astype(BF16)
    d = x.shape[-1]
    f = w1_ref.shape[-1]
    acc = jnp.zeros_like(x)
    for c in range(f // d):
        hid = jnp.maximum(_dot(hn, w1_ref[:, c * d:(c + 1) * d]), 0.0)
        acc = acc + _dot((hid * hid).astype(BF16), w2_ref[c * d:(c + 1) * d, :])
    o_ref[...] = x + m[5:6] * acc


def _mlp(st, x, g, mods, w1, w2, tm=256):
    tm = min(tm, st.l)
    return pl.pallas_call(
        _mlp_kernel,
        grid=st.grid(tm),
        in_specs=[st.tok_spec(tm), _const_spec((1, st.d)), st.mod_spec(),
                  _const_spec(w1.shape), _const_spec(w2.shape)],
        out_specs=st.tok_spec(tm),
        out_shape=st.shape(),
        compiler_params=_cparams(_PAR2),
        name="mlp",
    )(x, g.reshape(1, st.d), mods, w1, w2)


def _dft_mats(n):
    idx = jnp.arange(n, dtype=jnp.int32)
    ang = ((idx[:, None] * idx[None, :]) % n).astype(F32) * (2.0 * jnp.pi / n)
    return jnp.cos(ang), jnp.sin(ang)


def _fnet_chan_kernel(x_ref, g_ref, m_ref, c_ref, s_ref, o_ref):
    m = m_ref[...]
    d = x_ref.shape[-1]
    cg = d // FOURIER_GROUPS
    h = _norm_mod(x_ref[...], g_ref[...], m[0:1], m[1:2]).astype(BF16)
    for gi in range(FOURIER_GROUPS):
        hg = h[:, gi * cg:(gi + 1) * cg]
        o_ref[:, gi * cg:(gi + 1) * cg] = _dot(hg, c_ref[...]).astype(BF16)
        o_ref[:, d + gi * cg:d + (gi + 1) * cg] = _dot(hg, s_ref[...]).astype(BF16)


def _fnet_chan(st, x, g, mods, cc, sc, tm=256):
    return pl.pallas_call(
        _fnet_chan_kernel,
        grid=st.grid(tm),
        in_specs=[st.tok_spec(tm), _const_spec((1, st.d)), st.mod_spec(),
                  _const_spec(cc.shape), _const_spec(sc.shape)],
        out_specs=st.tok_spec(tm, 2 * st.d),
        out_shape=st.shape(2 * st.d, BF16),
        compiler_params=_cparams(_PAR2),
        name="fnet_chan",
    )(x, g.reshape(1, st.d), mods, cc, sc)


def _fnet_seq_kernel(scale, x_ref, xcs_ref, cl_ref, sl_ref, m_ref, wo_ref, o_ref, acc_ref):
    kstep = pl.program_id(2)
    d = x_ref.shape[-1]

    @pl.when(kstep == 0)
    def _():
        acc_ref[...] = jnp.zeros_like(acc_ref)

    acc_ref[...] += _dot(cl_ref[...], xcs_ref[:, :d]) - _dot(sl_ref[...], xcs_ref[:, d:])

    @pl.when(kstep == pl.num_programs(2) - 1)
    def _():
        f = (acc_ref[...] * scale).astype(BF16)
        o_ref[...] = x_ref[...] + m_ref[...][2:3] * _dot(f, wo_ref[...])


def _fnet_seq(st, x, xcs, mods, wo, tl=1024, tk=512):
    d, l = st.d, st.l
    tl = min(tl, l)
    tk = min(tk, l)
    cl, sl = _dft_mats(l)
    cl = cl.astype(BF16)
    sl = sl.astype(BF16)
    scale = 1.0 / float(l * (d // FOURIER_GROUPS)) ** 0.5
    return pl.pallas_call(
        functools.partial(_fnet_seq_kernel, scale),
        grid=(st.b, l // tl, l // tk),
        in_specs=[
            pl.BlockSpec((None, tl, d), lambda b, i, k: (b, i, 0)),
            pl.BlockSpec((None, tk, 2 * d), lambda b, i, k: (b, k, 0)),
            pl.BlockSpec((tl, tk), lambda b, i, k: (i, k)),
            pl.BlockSpec((tl, tk), lambda b, i, k: (i, k)),
            st.mod_spec(),
            _const_spec(wo.shape),
        ],
        out_specs=pl.BlockSpec((None, tl, d), lambda b, i, k: (b, i, 0)),
        out_shape=st.shape(),
        scratch_shapes=[pltpu.VMEM((tl, d), F32)],
        compiler_params=_cparams(("parallel", "parallel", "arbitrary")),
        name="fnet_seq",
    )(x, xcs, cl, sl, mods, wo)


def _final_kernel(x_ref, g_ref, o_ref):
    x = x_ref[...]
    ms = jnp.mean(x * x, axis=-1, keepdims=True)
    o_ref[...] = (x * lax.rsqrt(ms + RMS_EPS)) * g_ref[...]


def _final_norm(st, x, g, tm=256):
    return pl.pallas_call(
        _final_kernel,
        grid=st.grid(tm),
        in_specs=[st.tok_spec(tm), _const_spec((1, st.d))],
        out_specs=st.tok_spec(tm),
        out_shape=st.shape(),
        compiler_params=_cparams(_PAR2),
        name="final_norm",
    )(x, g.reshape(1, st.d))


def _blockdiag2(w):
    z = jnp.zeros_like(w[0])
    return jnp.concatenate([jnp.concatenate([w[0], z], axis=1),
                            jnp.concatenate([z, w[1]], axis=1)], axis=0)


def kernel(x, c, ctx, c_ctx, norm1_g, norm2_g, mod_w, mod_b, mlp_w1, mlp_w2, rk_mu, rk_wr, rk_wk, rk_wv, rk_wo, rk_w0, rk_w1, rk_w2, rk_a0, rk_a1, rk_a2, rk_v0, rk_v1, rk_v2, rk_g1, rk_g2, rk_kk, rk_ka, rk_rk, rk_lnx_w, rk_lnx_b, ft_wo, final_g):
    nb, seq, d = x.shape
    ctx_len = ctx.shape[1]
    depth = mod_w.shape[0]
    assert nb + 1 <= MOD_ROWS and d % LANES == 0
    lat = _Stream(nb, seq, d)
    cst = _Stream(nb, ctx_len, d, ctx_row=nb)
    row = lambda t: t.reshape(1, -1)

    cvec = jnp.zeros((MOD_ROWS, d), F32).at[:nb].set(c).at[nb].set(c_ctx)
    mods_all = _mod_table(cvec, mod_w, mod_b).reshape(depth, MOD_ROWS, N_MOD, d)

    heads = jnp.arange(d) // HEAD_DIM
    s1 = (heads[:, None] == jnp.arange(LANES)[None, :]).astype(BF16)
    s2 = s1.T
    cg = d // FOURIER_GROUPS
    cc, sc = _dft_mats(cg)
    cc = cc.astype(BF16)
    sc = sc.astype(BF16)
    zero_state = jnp.zeros((2, nb, d // LANES * HEAD_DIM, LANES), F32)

    last_a = ((depth - 1) // 2) * 2
    vf_lat = vf_ctx = None
    for i in range(depth):
        idx = i // 2
        mods = mods_all[i]
        ctx_in = i <= last_a
        ctx_out = i < last_a
        if i % 2 == 0:
            p = {
                'mu': rk_mu[idx], 'wr': rk_wr[idx].astype(BF16), 'wk': rk_wk[idx].astype(BF16),
                'wv': rk_wv[idx].astype(BF16), 'wo': rk_wo[idx].astype(BF16),
                'w1': jnp.concatenate([rk_w1[idx, 0], rk_w1[idx, 1]], axis=1).astype(BF16),
                'w2': _blockdiag2(rk_w2[idx]).astype(BF16), 'w0': row(rk_w0[idx]),
                'a1': jnp.concatenate([rk_a1[idx, 0], rk_a1[idx, 1]], axis=1).astype(BF16),
                'a2': _blockdiag2(rk_a2[idx]).astype(BF16), 'a0': row(rk_a0[idx]),
                'g1': rk_g1[idx].astype(BF16), 'g2': rk_g2[idx].astype(BF16),
                'kk': row(rk_kk[idx]), 'ka': row(rk_ka[idx]), 'rk': row(rk_rk[idx]),
                'lnx_w': row(rk_lnx_w[idx]), 'lnx_b': row(rk_lnx_b[idx]), 's1': s1, 's2': s2,
            }
            if idx > 0:
                p['v0'] = row(rk_v0[idx - 1])
                p['v1'] = rk_v1[idx - 1].astype(BF16)
                p['v2'] = rk_v2[idx - 1].astype(BF16)
            h_c = _prenorm(cst, ctx, norm1_g[i], mods)
            v_c, g_c, bonus_c, ykr_c, vr_c, a_c, w_c, k_c, b_c, q_c = _rwkv_proj(
                cst, h_c, _shift_ctx(h_c), vf_ctx if idx > 0 else None, p)
            y_c, s_ctx = _wkv(cst, a_c, vr_c, w_c, b_c, k_c, q_c, zero_state)
            h_l = _prenorm(lat, x, norm1_g[i], mods)
            v_l, g_l, bonus_l, ykr_l, vr_l, a_l, w_l, k_l, b_l, q_l = _rwkv_proj(
                lat, h_l, _qshift(h_l), vf_lat if idx > 0 else None, p)
            y_l, _ = _wkv(lat, a_l, vr_l, w_l, b_l, k_l, q_l, s_ctx)
            if idx == 0:
                vf_ctx, vf_lat = v_c, v_l
            x = _rwkv_out(lat, x, y_l, ykr_l, bonus_l, g_l, mods, p)
            if ctx_out:
                ctx = _rwkv_out(cst, ctx, y_c, ykr_c, bonus_c, g_c, mods, p)
        else:
            wo = ft_wo[idx].astype(BF16)
            x = _fnet_seq(lat, x, _fnet_chan(lat, x, norm1_g[i], mods, cc, sc), mods, wo)
            if ctx_out:
                ctx = _fnet_seq(cst, ctx, _fnet_chan(cst, ctx, norm1_g[i], mods, cc, sc), mods, wo)
        w1 = mlp_w1[i].astype(BF16)
        w2 = mlp_w2[i].astype(BF16)
        x = _mlp(lat, x, norm2_g[i], mods, w1, w2, tm=512)
        if ctx_out:
            ctx = _mlp(cst, ctx, norm2_g[i], mods, w1, w2)
    return _final_norm(lat, x, final_g)
```

```python
import functools

import jax
import jax.numpy as jnp
from jax import lax
from jax.experimental import pallas as pl
from jax.experimental.pallas import tpu as pltpu

HEAD_DIM = 64
GRID_W = 64
N_MOD = 6
FOURIER_GROUPS = 4
RMS_EPS = 1e-6
LNX_EPS = 64e-5
L2_EPS = 1e-24

MOD_ROWS = 16
LANES = 128
GROUP = 256
CHUNK = 64
VMEM_LIMIT = 56 * 1024 * 1024

F32 = jnp.float32
BF16 = jnp.bfloat16


def _cparams(sem):
    return pltpu.CompilerParams(dimension_semantics=sem, vmem_limit_bytes=VMEM_LIMIT)


def _const_spec(shape):
    nd = len(shape)
    return pl.BlockSpec(shape, lambda *_: (0,) * nd, pipeline_mode=pl.Buffered(1))


def _dot(a, b):
    return jnp.dot(a, b, preferred_element_type=F32)


def _split2(x):
    hi = x.astype(BF16)
    lo = (x - hi.astype(F32)).astype(BF16)
    return hi, lo


def _head_sum(x, s1, s2):
    hi, lo = _split2(x)
    s = _dot(hi, s1) + _dot(lo, s1)
    h1 = s.astype(BF16)
    r1 = s - h1.astype(F32)
    h2 = r1.astype(BF16)
    h3 = (r1 - h2.astype(F32)).astype(BF16)
    return _dot(h1, s2) + _dot(h2, s2) + _dot(h3, s2)


def _norm_mod(x, g, sh, sc):
    ms = jnp.mean(x * x, axis=-1, keepdims=True)
    return (x * lax.rsqrt(ms + RMS_EPS)) * g * (1.0 + sc) + sh


def _mod_kernel(c_ref, w_ref, b_ref, o_ref):
    c = c_ref[...]
    s = c * jax.nn.sigmoid(c)
    o_ref[...] = jnp.dot(s, w_ref[...], precision=lax.Precision.HIGHEST,
                         preferred_element_type=F32) + b_ref[...]


def _mod_table(cvec, mod_w, mod_b):
    depth, d, nd = mod_w.shape
    nt = nd // d
    return pl.pallas_call(
        _mod_kernel,
        grid=(depth, nt),
        in_specs=[
            pl.BlockSpec((MOD_ROWS, d), lambda l, j: (0, 0)),
            pl.BlockSpec((None, d, d), lambda l, j: (l, 0, j)),
            pl.BlockSpec((None, 1, d), lambda l, j: (l, 0, j)),
        ],
        out_specs=pl.BlockSpec((None, MOD_ROWS, d), lambda l, j: (l, 0, j)),
        out_shape=jax.ShapeDtypeStruct((depth, MOD_ROWS, nd), F32),
        compiler_params=_cparams(("arbitrary", "arbitrary")),
        name="mod_table",
    )(cvec, mod_w, mod_b.reshape(depth, 1, nd))


class _Stream:
    def __init__(self, b, l, d, ctx_row=None):
        self.b, self.l, self.d, self.ctx_row = b, l, d, ctx_row

    def grid(self, tm):
        assert self.l % tm == 0
        return (self.b, self.l // tm)

    def tok_spec(self, tm, width=None):
        return pl.BlockSpec((None, tm, width or self.d), lambda b, i: (b, i, 0))

    def dir_spec(self, tm):
        return pl.BlockSpec((2, None, tm, self.d), lambda b, i: (0, b, i, 0))

    def dir_shape(self):
        return jax.ShapeDtypeStruct((2, self.b, self.l, self.d), F32)

    def mod_row(self, b):
        return b if self.ctx_row is None else b * 0 + self.ctx_row

    def mod_spec(self):
        return pl.BlockSpec((None, N_MOD, self.d), lambda b, *_: (self.mod_row(b), 0, 0))

    def shape(self, width=None, dtype=F32):
        return jax.ShapeDtypeStruct((self.b, self.l, width or self.d), dtype)


_PAR2 = ("parallel", "parallel")


def _prenorm_kernel(x_ref, g_ref, m_ref, o_ref):
    m = m_ref[...]
    o_ref[...] = _norm_mod(x_ref[...], g_ref[...], m[0:1], m[1:2])


def _prenorm(st, x, g, mods, tm=256):
    return pl.pallas_call(
        _prenorm_kernel,
        grid=st.grid(tm),
        in_specs=[st.tok_spec(tm), _const_spec((1, st.d)), st.mod_spec()],
        out_specs=st.tok_spec(tm),
        out_shape=st.shape(),
        compiler_params=_cparams(_PAR2),
        name="prenorm",
    )(x, g.reshape(1, st.d), mods)


def _qshift(h):
    b, t, d = h.shape
    rows = t // GRID_W
    g = h.reshape(b, rows, GRID_W, 4, d // 4)
    left = jnp.pad(g[:, :, :-1, 0], ((0, 0), (0, 0), (1, 0), (0, 0)))
    right = jnp.pad(g[:, :, 1:, 1], ((0, 0), (0, 0), (0, 1), (0, 0)))
    up = jnp.pad(g[:, :-1, :, 2], ((0, 0), (1, 0), (0, 0), (0, 0)))
    down = jnp.pad(g[:, 1:, :, 3], ((0, 0), (0, 1), (0, 0), (0, 0)))
    return jnp.stack([left, right, up, down], axis=3).reshape(b, t, d)


def _shift_ctx(h):
    d = h.shape[-1]
    prev = jnp.pad(h[:, :-1, : d // 2], ((0, 0), (1, 0), (0, 0)))
    nxt = jnp.pad(h[:, 1:, d // 2:], ((0, 0), (0, 1), (0, 0)))
    return jnp.concatenate([prev, nxt], axis=-1)


def _rwkv_proj_kernel(has_vres, *refs):
    if has_vres:
        (h_ref, s_ref, vf_ref, mu_ref, wr_ref, wk_ref, wv_ref, w1_ref, w2_ref, w0_ref,
         a1_ref, a2_ref, a0_ref, g1_ref, g2_ref, kk_ref, ka_ref, rk_ref, s1_ref, s2_ref,
         v0_ref, v1_ref, v2_ref,
         v_o, g_o, bonus_o, a_o, r_o, lw_o, k_o, b_o) = refs
    else:
        (h_ref, s_ref, mu_ref, wr_ref, wk_ref, wv_ref, w1_ref, w2_ref, w0_ref,
         a1_ref, a2_ref, a0_ref, g1_ref, g2_ref, kk_ref, ka_ref, rk_ref, s1_ref, s2_ref,
         v_o, g_o, bonus_o, a_o, r_o, lw_o, k_o, b_o) = refs
    d = h_ref.shape[-1]
    h = h_ref[...]
    xx = s_ref[...] - h
    mu = mu_ref[...]
    s1 = s1_ref[...]
    s2 = s2_ref[...]

    def mix(j):
        return (h + xx * mu[j:j + 1]).astype(BF16)

    r = _dot(mix(0), wr_ref[...])
    k = _dot(mix(2), wk_ref[...])
    xv = mix(3)
    v = _dot(xv, wv_ref[...])
    if has_vres:
        lo = _dot(_dot(xv, v1_ref[...]).astype(BF16), v2_ref[...])
        v = v + (vf_ref[...] - v) * jax.nn.sigmoid(v0_ref[...] + lo)
    kk = k * kk_ref[...]
    kk = kk * lax.rsqrt(jnp.maximum(_head_sum(kk * kk, s1, s2), L2_EPS))
    wl = jnp.tanh(_dot(mix(1), w1_ref[...])).astype(BF16)
    wd = w0_ref[...] + _dot(wl, w2_ref[...])
    al = _dot(mix(4), a1_ref[...]).astype(BF16)
    iclr2 = jax.nn.sigmoid(a0_ref[...] + _dot(al, a2_ref[...]))
    g = _dot(jax.nn.sigmoid(_dot(mix(5), g1_ref[...])).astype(BF16), g2_ref[...])
    ka = ka_ref[...]
    rk = rk_ref[...]
    a = -kk
    v_o[...] = v
    g_o[...] = g
    a_o[...] = a
    r_o[...] = r
    bonus = jnp.zeros_like(r)
    for dr in range(2):
        z = -wd[:, dr * d:(dr + 1) * d]
        softplus = jnp.maximum(z, 0.0) + jnp.log1p(jnp.exp(-jnp.abs(z)))
        iclr = iclr2[:, dr * d:(dr + 1) * d]
        kd = k * (1.0 + (iclr - 1.0) * ka)
        lw_o[dr] = -jnp.exp(-softplus - 0.5)
        k_o[dr] = kd
        b_o[dr] = kk * iclr
        bonus = bonus + r * kd * rk
    bonus_o[...] = _head_sum(bonus, s1, s2) * v


def _rwkv_proj(st, h, shifted, vfirst, p, tm=256):
    has_vres = vfirst is not None
    tok = st.tok_spec(tm)
    ins = [h, shifted] + ([vfirst] if has_vres else [])
    specs = [tok, tok] + ([tok] if has_vres else [])
    consts = [p['mu'], p['wr'], p['wk'], p['wv'], p['w1'], p['w2'], p['w0'],
              p['a1'], p['a2'], p['a0'], p['g1'], p['g2'], p['kk'], p['ka'], p['rk'],
              p['s1'], p['s2']]
    if has_vres:
        consts += [p['v0'], p['v1'], p['v2']]
    ins += consts
    specs += [_const_spec(c.shape) for c in consts]
    return pl.pallas_call(
        functools.partial(_rwkv_proj_kernel, has_vres),
        grid=st.grid(tm),
        in_specs=specs,
        out_specs=[tok] * 5 + [st.dir_spec(tm)] * 3,
        out_shape=[st.shape()] * 5 + [st.dir_shape()] * 3,
        compiler_params=_cparams(_PAR2),
        name="rwkv_proj",
    )(*ins)


def _dot_nt(a, b):
    return lax.dot_general(a, b, (((1,), (1,)), ((), ())), preferred_element_type=F32)


def _split3(x):
    h1 = x.astype(BF16)
    r1 = x - h1.astype(F32)
    h2 = r1.astype(BF16)
    h3 = (r1 - h2.astype(F32)).astype(BF16)
    return h1, h2, h3


def _wkv_chunk_kernel(nb, a_ref, r_ref, v_ref, lw_ref, k_ref, b_ref, s0_ref, m4_ref, i4_ref,
                      y_ref, sf_ref, s_ref):
    dr = pl.program_id(0)
    ch = pl.program_id(1)
    ngroup = a_ref.shape[-1] // GROUP
    rev = dr == 1

    @pl.when(ch == 0)
    def _():
        s_ref[...] = s0_ref[...]

    m4 = m4_ref[...]
    i4 = i4_ref[...]
    row = lax.broadcasted_iota(jnp.int32, (CHUNK, GROUP), 0)
    col = lax.broadcasted_iota(jnp.int32, (CHUNK, GROUP), 1) % HEAD_DIM
    ahead = jnp.where(rev, row - col, col - row)
    strict = (ahead > 0).astype(F32)
    incl = (ahead >= 0).astype(F32)
    tr = lax.broadcasted_iota(jnp.int32, (CHUNK, CHUNK), 0)
    tc = lax.broadcasted_iota(jnp.int32, (CHUNK, CHUNK), 1)
    tri = (jnp.where(rev, tc - tr, tr - tc) >= 0).astype(F32).astype(BF16)

    def bd(x):
        xb = x.astype(BF16)
        return jnp.concatenate([xb, xb, xb, xb], axis=0) * m4

    def body(bi, carry):
        gs = range(ngroup)
        sl = [slice(g * GROUP, (g + 1) * GROUP) for g in gs]
        a = [a_ref[bi, :, sl[g]] for g in gs]
        r = [r_ref[bi, :, sl[g]] for g in gs]
        v = [v_ref[bi, :, sl[g]] for g in gs]
        lw = [lw_ref[bi, :, sl[g]] for g in gs]
        k = [k_ref[bi, :, sl[g]] for g in gs]
        b = [b_ref[bi, :, sl[g]] for g in gs]
        s0 = [s_ref[bi, g] for g in gs]
        cum = []
        for g in gs:
            l1, l2, l3 = _split3(lw[g])
            cum.append(_dot(tri, l1) + _dot(tri, l2) + _dot(tri, l3))
        at = [a[g] * jnp.exp(cum[g] - lw[g]) for g in gs]
        rt = [r[g] * jnp.exp(cum[g]) for g in gs]
        ginv = [jnp.exp(-cum[g]) for g in gs]
        bt = [b[g] * ginv[g] for g in gs]
        kt = [k[g] * ginv[g] for g in gs]
        gtot = [jnp.exp(jnp.where(rev, cum[g][0:1], cum[g][CHUNK - 1:CHUNK])) for g in gs]
        z = [jnp.concatenate([bt[g], kt[g], s0[g]], axis=0).astype(BF16) for g in gs]
        ma = [_dot_nt(z[g], bd(at[g])) for g in gs]
        mr = [_dot_nt(z[g], bd(rt[g])) for g in gs]
        vt = [_dot_nt(i4, bd(v[g])) for g in gs]
        q = [ma[g][0:CHUNK] * strict for g in gs]
        u = [ma[g][2 * CHUNK:] + _dot(vt[g].astype(BF16), bd(ma[g][CHUNK:2 * CHUNK] * strict)) for g in gs]
        for j in range(6):
            for g in gs:
                if j < 5:
                    p = _dot(jnp.concatenate([u[g], q[g]], axis=0).astype(BF16), bd(q[g]))
                    u[g] = u[g] + p[0:CHUNK]
                    q[g] = p[CHUNK:]
                else:
                    u[g] = u[g] + _dot(u[g].astype(BF16), bd(q[g]))
        uv = [jnp.concatenate([u[g], vt[g]], axis=1).astype(BF16) for g in gs]
        yt = [mr[g][2 * CHUNK:] + _dot(uv[g], jnp.concatenate(
            [bd(mr[g][0:CHUNK] * incl), bd(mr[g][CHUNK:2 * CHUNK] * incl)], axis=0)) for g in gs]
        sn = [(s0[g] + _dot(uv[g], jnp.concatenate([bd(bt[g]), bd(kt[g])], axis=0))) * gtot[g]
              for g in gs]
        y = []
        for g in gs:
            yh = yt[g].astype(BF16)
            yl = yt[g] - yh.astype(F32)
            y.append(_dot_nt(i4, bd(yh)) + _dot_nt(i4, bd(yl)))
        for g in gs:
            s_ref[bi, g] = sn[g]
            y_ref[bi, :, sl[g]] = y[g]
        return carry

    lax.fori_loop(0, nb, body, 0)

    @pl.when(ch == pl.num_programs(1) - 1)
    def _():
        sf_ref[...] = s_ref[...]


def _wkv(st, a, r, v, lw, k, b, s0):
    nb, l, d = st.b, st.l, st.d
    ngroup = d // GROUP
    nch = l // CHUNK
    assert l % CHUNK == 0 and d % GROUP == 0

    def chunk(dr, j):
        return jnp.where(dr == 1, nch - 1 - j, j)

    sh_spec = pl.BlockSpec((nb, CHUNK, d), lambda dr, j: (0, chunk(dr, j), 0))
    dir_spec = pl.BlockSpec((None, nb, CHUNK, d), lambda dr, j: (dr, 0, chunk(dr, j), 0))
    st_spec = pl.BlockSpec((None, nb, ngroup, HEAD_DIM, GROUP), lambda dr, j: (dr, 0, 0, 0, 0))
    blk = jnp.arange(GROUP) // HEAD_DIM
    m4 = (blk[:, None] == blk[None, :]).astype(BF16)
    i4 = (jnp.arange(HEAD_DIM)[:, None] == (jnp.arange(GROUP)[None, :] % HEAD_DIM)).astype(BF16)
    return pl.pallas_call(
        functools.partial(_wkv_chunk_kernel, nb),
        grid=(2, nch),
        in_specs=[sh_spec, sh_spec, sh_spec, dir_spec, dir_spec, dir_spec, st_spec,
                  _const_spec(m4.shape), _const_spec(i4.shape)],
        out_specs=[dir_spec, st_spec],
        out_shape=[jax.ShapeDtypeStruct((2, nb, l, d), F32), jax.ShapeDtypeStruct(s0.shape, F32)],
        scratch_shapes=[pltpu.VMEM((nb, ngroup, HEAD_DIM, GROUP), F32)],
        compiler_params=_cparams(("arbitrary", "arbitrary")),
        name="wkv",
    )(a, r, v, lw, k, b, s0, m4, i4)


def _rwkv_out_kernel(x_ref, y_ref, bonus_ref, g_ref, m_ref, lw_ref, lb_ref, wo_ref,
                     s1_ref, s2_ref, o_ref):
    s1 = s1_ref[...]
    s2 = s2_ref[...]
    y = y_ref[0] + y_ref[1]
    mean = _head_sum(y, s1, s2) * (1.0 / HEAD_DIM)
    yc = y - mean
    var = _head_sum(yc * yc, s1, s2) * (1.0 / HEAD_DIM)
    yn = (yc * lax.rsqrt(var + LNX_EPS)) * lw_ref[...] + lb_ref[...]
    out = (yn + bonus_ref[...]) * g_ref[...]
    o = _dot(out.astype(BF16), wo_ref[...])
    o_ref[...] = x_ref[...] + m_ref[...][2:3] * o


def _rwkv_out(st, x, y, bonus, g, mods, p, tm=256):
    tok = st.tok_spec(tm)
    consts = [p['lnx_w'], p['lnx_b'], p['wo'], p['s1'], p['s2']]
    return pl.pallas_call(
        _rwkv_out_kernel,
        grid=st.grid(tm),
        in_specs=[tok, st.dir_spec(tm), tok, tok, st.mod_spec()]
        + [_const_spec(c.shape) for c in consts],
        out_specs=tok,
        out_shape=st.shape(),
        compiler_params=_cparams(_PAR2),
        name="rwkv_out",
    )(x, y, bonus, g, mods, *consts)


def _mlp_kernel(x_ref, g_ref, m_ref, w1_ref, w2_ref, o_ref):
    x = x_ref[...]
    m = m_ref[...]
    hn = _norm_mod(x, g_ref[...], m[3:4], m[4:5]).astype(BF16)
    d = x.shape[-1]
    f = w1_ref.shape[-1]
    acc = jnp.zeros_like(x)
    for c in range(f // d):
        hid = jnp.maximum(_dot(hn, w1_ref[:, c * d:(c + 1) * d]), 0.0)
        acc = acc + _dot((hid * hid).astype(BF16), w2_ref[c * d:(c + 1) * d, :])
    o_ref[...] = x + m[5:6] * acc


def _mlp(st, x, g, mods, w1, w2, tm=256):
    tm = min(tm, st.l)
    return pl.pallas_call(
        _mlp_kernel,
        grid=st.grid(tm),
        in_specs=[st.tok_spec(tm), _const_spec((1, st.d)), st.mod_spec(),
                  _const_spec(w1.shape), _const_spec(w2.shape)],
        out_specs=st.tok_spec(tm),
        out_shape=st.shape(),
        compiler_params=_cparams(_PAR2),
        name="mlp",
    )(x, g.reshape(1, st.d), mods, w1, w2)


def _dft_mats(n):
    idx = jnp.arange(n, dtype=jnp.int32)
    ang = ((idx[:, None] * idx[None, :]) % n).astype(F32) * (2.0 * jnp.pi / n)
    return jnp.cos(ang), jnp.sin(ang)


def _fnet_chan_kernel(x_ref, g_ref, m_ref, c_ref, s_ref, o_ref):
    m = m_ref[...]
    d = x_ref.shape[-1]
    cg = d // FOURIER_GROUPS
    h = _norm_mod(x_ref[...], g_ref[...], m[0:1], m[1:2]).astype(BF16)
    for gi in range(FOURIER_GROUPS):
        hg = h[:, gi * cg:(gi + 1) * cg]
        o_ref[:, gi * cg:(gi + 1) * cg] = _dot(hg, c_ref[...]).astype(BF16)
        o_ref[:, d + gi * cg:d + (gi + 1) * cg] = _dot(hg, s_ref[...]).astype(BF16)


def _fnet_chan(st, x, g, mods, cc, sc, tm=256):
    return pl.pallas_call(
        _fnet_chan_kernel,
        grid=st.grid(tm),
        in_specs=[st.tok_spec(tm), _const_spec((1, st.d)), st.mod_spec(),
                  _const_spec(cc.shape), _const_spec(sc.shape)],
        out_specs=st.tok_spec(tm, 2 * st.d),
        out_shape=st.shape(2 * st.d, BF16),
        compiler_params=_cparams(_PAR2),
        name="fnet_chan",
    )(x, g.reshape(1, st.d), mods, cc, sc)


def _fnet_seq_kernel(scale, x_ref, xcs_ref, cl_ref, sl_ref, m_ref, wo_ref, o_ref, acc_ref):
    kstep = pl.program_id(2)
    d = x_ref.shape[-1]

    @pl.when(kstep == 0)
    def _():
        acc_ref[...] = jnp.zeros_like(acc_ref)

    acc_ref[...] += _dot(cl_ref[...], xcs_ref[:, :d]) - _dot(sl_ref[...], xcs_ref[:, d:])

    @pl.when(kstep == pl.num_programs(2) - 1)
    def _():
        f = (acc_ref[...] * scale).astype(BF16)
        o_ref[...] = x_ref[...] + m_ref[...][2:3] * _dot(f, wo_ref[...])


def _fnet_seq(st, x, xcs, mods, wo, tl=1024, tk=512):
    d, l = st.d, st.l
    tl = min(tl, l)
    tk = min(tk, l)
    cl, sl = _dft_mats(l)
    cl = cl.astype(BF16)
    sl = sl.astype(BF16)
    scale = 1.0 / float(l * (d // FOURIER_GROUPS)) ** 0.5
    return pl.pallas_call(
        functools.partial(_fnet_seq_kernel, scale),
        grid=(st.b, l // tl, l // tk),
        in_specs=[
            pl.BlockSpec((None, tl, d), lambda b, i, k: (b, i, 0)),
            pl.BlockSpec((None, tk, 2 * d), lambda b, i, k: (b, k, 0)),
            pl.BlockSpec((tl, tk), lambda b, i, k: (i, k)),
            pl.BlockSpec((tl, tk), lambda b, i, k: (i, k)),
            st.mod_spec(),
            _const_spec(wo.shape),
        ],
        out_specs=pl.BlockSpec((None, tl, d), lambda b, i, k: (b, i, 0)),
        out_shape=st.shape(),
        scratch_shapes=[pltpu.VMEM((tl, d), F32)],
        compiler_params=_cparams(("parallel", "parallel", "arbitrary")),
        name="fnet_seq",
    )(x, xcs, cl, sl, mods, wo)


def _final_kernel(x_ref, g_ref, o_ref):
    x = x_ref[...]
    ms = jnp.mean(x * x, axis=-1, keepdims=True)
    o_ref[...] = (x * lax.rsqrt(ms + RMS_EPS)) * g_ref[...]


def _final_norm(st, x, g, tm=256):
    return pl.pallas_call(
        _final_kernel,
        grid=st.grid(tm),
        in_specs=[st.tok_spec(tm), _const_spec((1, st.d))],
        out_specs=st.tok_spec(tm),
        out_shape=st.shape(),
        compiler_params=_cparams(_PAR2),
        name="final_norm",
    )(x, g.reshape(1, st.d))


def _blockdiag2(w):
    z = jnp.zeros_like(w[0])
    return jnp.concatenate([jnp.concatenate([w[0], z], axis=1),
                            jnp.concatenate([z, w[1]], axis=1)], axis=0)


def kernel(x, c, ctx, c_ctx, norm1_g, norm2_g, mod_w, mod_b, mlp_w1, mlp_w2, rk_mu, rk_wr, rk_wk, rk_wv, rk_wo, rk_w0, rk_w1, rk_w2, rk_a0, rk_a1, rk_a2, rk_v0, rk_v1, rk_v2, rk_g1, rk_g2, rk_kk, rk_ka, rk_rk, rk_lnx_w, rk_lnx_b, ft_wo, final_g):
    nb, seq, d = x.shape
    ctx_len = ctx.shape[1]
    depth = mod_w.shape[0]
    assert nb + 1 <= MOD_ROWS and d % LANES == 0
    lat = _Stream(nb, seq, d)
    cst = _Stream(nb, ctx_len, d, ctx_row=nb)
    row = lambda t: t.reshape(1, -1)

    cvec = jnp.zeros((MOD_ROWS, d), F32).at[:nb].set(c).at[nb].set(c_ctx)
    mods_all = _mod_table(cvec, mod_w, mod_b).reshape(depth, MOD_ROWS, N_MOD, d)

    heads = jnp.arange(d) // HEAD_DIM
    s1 = (heads[:, None] == jnp.arange(LANES)[None, :]).astype(BF16)
    s2 = s1.T
    cg = d // FOURIER_GROUPS
    cc, sc = _dft_mats(cg)
    cc = cc.astype(BF16)
    sc = sc.astype(BF16)
    zero_state = jnp.zeros((2, nb, d // GROUP, HEAD_DIM, GROUP), F32)

    last_a = ((depth - 1) // 2) * 2
    vf_lat = vf_ctx = None
    for i in range(depth):
        idx = i // 2
        mods = mods_all[i]
        ctx_in = i <= last_a
        ctx_out = i < last_a
        if i % 2 == 0:
            p = {
                'mu': rk_mu[idx], 'wr': rk_wr[idx].astype(BF16), 'wk': rk_wk[idx].astype(BF16),
                'wv': rk_wv[idx].astype(BF16), 'wo': rk_wo[idx].astype(BF16),
                'w1': jnp.concatenate([rk_w1[idx, 0], rk_w1[idx, 1]], axis=1).astype(BF16),
                'w2': _blockdiag2(rk_w2[idx]).astype(BF16), 'w0': row(rk_w0[idx]),
                'a1': jnp.concatenate([rk_a1[idx, 0], rk_a1[idx, 1]], axis=1).astype(BF16),
                'a2': _blockdiag2(rk_a2[idx]).astype(BF16), 'a0': row(rk_a0[idx]),
                'g1': rk_g1[idx].astype(BF16), 'g2': rk_g2[idx].astype(BF16),
                'kk': row(rk_kk[idx]), 'ka': row(rk_ka[idx]), 'rk': row(rk_rk[idx]),
                'lnx_w': row(rk_lnx_w[idx]), 'lnx_b': row(rk_lnx_b[idx]), 's1': s1, 's2': s2,
            }
            if idx > 0:
                p['v0'] = row(rk_v0[idx - 1])
                p['v1'] = rk_v1[idx - 1].astype(BF16)
                p['v2'] = rk_v2[idx - 1].astype(BF16)
            h_c = _prenorm(cst, ctx, norm1_g[i], mods)
            v_c, g_c, bonus_c, a_c, r_c, lw_c, k_c, b_c = _rwkv_proj(
                cst, h_c, _shift_ctx(h_c), vf_ctx if idx > 0 else None, p)
            y_c, s_ctx = _wkv(cst, a_c, r_c, v_c, lw_c, k_c, b_c, zero_state)
            h_l = _prenorm(lat, x, norm1_g[i], mods)
            v_l, g_l, bonus_l, a_l, r_l, lw_l, k_l, b_l = _rwkv_proj(
                lat, h_l, _qshift(h_l), vf_lat if idx > 0 else None, p)
            y_l, _ = _wkv(lat, a_l, r_l, v_l, lw_l, k_l, b_l, s_ctx)
            if idx == 0:
                vf_ctx, vf_lat = v_c, v_l
            x = _rwkv_out(lat, x, y_l, bonus_l, g_l, mods, p)
            if ctx_out:
                ctx = _rwkv_out(cst, ctx, y_c, bonus_c, g_c, mods, p)
        else:
            wo = ft_wo[idx].astype(BF16)
            x = _fnet_seq(lat, x, _fnet_chan(lat, x, norm1_g[i], mods, cc, sc), mods, wo)
            if ctx_out:
                ctx = _fnet_seq(cst, ctx, _fnet_chan(cst, ctx, norm1_g[i], mods, cc, sc), mods, wo)
        w1 = mlp_w1[i].astype(BF16)
        w2 = mlp_w2[i].astype(BF16)
        x = _mlp(lat, x, norm2_g[i], mods, w1, w2, tm=512)
        if ctx_out:
            ctx = _mlp(cst, ctx, norm2_g[i], mods, w1, w2)
    return _final_norm(lat, x, final_g)
```

```python
import functools

import jax
import jax.numpy as jnp
from jax import lax
from jax.experimental import pallas as pl
from jax.experimental.pallas import tpu as pltpu

HEAD_DIM = 64
GRID_W = 64
N_MOD = 6
FOURIER_GROUPS = 4
RMS_EPS = 1e-6
LNX_EPS = 64e-5
L2_EPS = 1e-24

MOD_ROWS = 16
LANES = 128
GROUP = 256
CHUNK = 64
WKV_NB = 2
VMEM_LIMIT = 56 * 1024 * 1024

F32 = jnp.float32
BF16 = jnp.bfloat16


def _cparams(sem):
    return pltpu.CompilerParams(dimension_semantics=sem, vmem_limit_bytes=VMEM_LIMIT)


def _const_spec(shape):
    nd = len(shape)
    return pl.BlockSpec(shape, lambda *_: (0,) * nd, pipeline_mode=pl.Buffered(1))


def _dot(a, b):
    return jnp.dot(a, b, preferred_element_type=F32)


def _split2(x):
    hi = x.astype(BF16)
    lo = (x - hi.astype(F32)).astype(BF16)
    return hi, lo


def _head_sum(x, s1, s2):
    hi, lo = _split2(x)
    s = _dot(hi, s1) + _dot(lo, s1)
    h1 = s.astype(BF16)
    r1 = s - h1.astype(F32)
    h2 = r1.astype(BF16)
    h3 = (r1 - h2.astype(F32)).astype(BF16)
    return _dot(h1, s2) + _dot(h2, s2) + _dot(h3, s2)


def _norm_mod(x, g, sh, sc):
    ms = jnp.mean(x * x, axis=-1, keepdims=True)
    return (x * lax.rsqrt(ms + RMS_EPS)) * g * (1.0 + sc) + sh


def _mod_kernel(c_ref, w_ref, b_ref, o_ref):
    c = c_ref[...]
    s = c * jax.nn.sigmoid(c)
    o_ref[...] = jnp.dot(s, w_ref[...], precision=lax.Precision.HIGHEST,
                         preferred_element_type=F32) + b_ref[...]


def _mod_table(cvec, mod_w, mod_b):
    depth, d, nd = mod_w.shape
    nt = nd // d
    return pl.pallas_call(
        _mod_kernel,
        grid=(depth, nt),
        in_specs=[
            pl.BlockSpec((MOD_ROWS, d), lambda l, j: (0, 0)),
            pl.BlockSpec((None, d, d), lambda l, j: (l, 0, j)),
            pl.BlockSpec((None, 1, d), lambda l, j: (l, 0, j)),
        ],
        out_specs=pl.BlockSpec((None, MOD_ROWS, d), lambda l, j: (l, 0, j)),
        out_shape=jax.ShapeDtypeStruct((depth, MOD_ROWS, nd), F32),
        compiler_params=_cparams(("arbitrary", "arbitrary")),
        name="mod_table",
    )(cvec, mod_w, mod_b.reshape(depth, 1, nd))


class _Stream:
    def __init__(self, b, l, d, ctx_row=None):
        self.b, self.l, self.d, self.ctx_row = b, l, d, ctx_row

    def grid(self, tm):
        assert self.l % tm == 0
        return (self.b, self.l // tm)

    def tok_spec(self, tm, width=None):
        return pl.BlockSpec((None, tm, width or self.d), lambda b, i: (b, i, 0))

    def dir_spec(self, tm):
        return pl.BlockSpec((2, None, tm, self.d), lambda b, i: (0, b, i, 0))

    def dir_shape(self):
        return jax.ShapeDtypeStruct((2, self.b, self.l, self.d), F32)

    def mod_row(self, b):
        return b if self.ctx_row is None else b * 0 + self.ctx_row

    def mod_spec(self):
        return pl.BlockSpec((None, N_MOD, self.d), lambda b, *_: (self.mod_row(b), 0, 0))

    def shape(self, width=None, dtype=F32):
        return jax.ShapeDtypeStruct((self.b, self.l, width or self.d), dtype)


_PAR2 = ("parallel", "parallel")


def _grid_shift(h, h_up, h_down, grid_w):
    tm, d = h.shape
    q = d // 4
    col = lax.broadcasted_iota(jnp.int32, (tm, q), 0) % grid_w
    left = jnp.where(col == 0, 0.0, pltpu.roll(h[:, 0:q], 1, axis=0))
    right = jnp.where(col == grid_w - 1, 0.0, pltpu.roll(h[:, q:2 * q], tm - 1, axis=0))
    up = jnp.concatenate([h_up[:, 2 * q:3 * q], h[:tm - grid_w, 2 * q:3 * q]], axis=0)
    down = jnp.concatenate([h[grid_w:, 3 * q:], h_down[:, 3 * q:]], axis=0)
    return jnp.concatenate([left, right, up, down], axis=1)


def _line_shift(h):
    tm, d = h.shape
    half = d // 2
    t = lax.broadcasted_iota(jnp.int32, (tm, half), 0)
    prev = jnp.where(t == 0, 0.0, pltpu.roll(h[:, :half], 1, axis=0))
    nxt = jnp.where(t == tm - 1, 0.0, pltpu.roll(h[:, half:], tm - 1, axis=0))
    return jnp.concatenate([prev, nxt], axis=1)


def _rwkv_proj_kernel(has_vres, grid_w, *refs):
    if has_vres:
        (x_ref, xu_ref, xd_ref, ng_ref, m_ref, vf_ref, mu_ref, wr_ref, wk_ref, wv_ref, w1_ref, w2_ref, w0_ref,
         a1_ref, a2_ref, a0_ref, g1_ref, g2_ref, kk_ref, ka_ref, rk_ref, s1_ref, s2_ref,
         v0_ref, v1_ref, v2_ref,
         v_o, g_o, bonus_o, a_o, r_o, lw_o, k_o, b_o) = refs
    else:
        (x_ref, xu_ref, xd_ref, ng_ref, m_ref, mu_ref, wr_ref, wk_ref, wv_ref, w1_ref, w2_ref, w0_ref,
         a1_ref, a2_ref, a0_ref, g1_ref, g2_ref, kk_ref, ka_ref, rk_ref, s1_ref, s2_ref,
         v_o, g_o, bonus_o, a_o, r_o, lw_o, k_o, b_o) = refs
    d = x_ref.shape[-1]
    m = m_ref[...]
    ng = ng_ref[...]
    h = _norm_mod(x_ref[...], ng, m[0:1], m[1:2])
    if grid_w:
        i = pl.program_id(1)
        h_up = jnp.where(i > 0, _norm_mod(xu_ref[...], ng, m[0:1], m[1:2]), 0.0)
        h_down = jnp.where(i < pl.num_programs(1) - 1, _norm_mod(xd_ref[...], ng, m[0:1], m[1:2]), 0.0)
        xx = _grid_shift(h, h_up, h_down, grid_w) - h
    else:
        xx = _line_shift(h) - h
    mu = mu_ref[...]
    s1 = s1_ref[...]
    s2 = s2_ref[...]

    def mix(j):
        return (h + xx * mu[j:j + 1]).astype(BF16)

    r = _dot(mix(0), wr_ref[...])
    k = _dot(mix(2), wk_ref[...])
    xv = mix(3)
    v = _dot(xv, wv_ref[...])
    if has_vres:
        lo = _dot(_dot(xv, v1_ref[...]).astype(BF16), v2_ref[...])
        v = v + (vf_ref[...] - v) * jax.nn.sigmoid(v0_ref[...] + lo)
    kk = k * kk_ref[...]
    kk = kk * lax.rsqrt(jnp.maximum(_head_sum(kk * kk, s1, s2), L2_EPS))
    wl = jnp.tanh(_dot(mix(1), w1_ref[...])).astype(BF16)
    wd = w0_ref[...] + _dot(wl, w2_ref[...])
    al = _dot(mix(4), a1_ref[...]).astype(BF16)
    iclr2 = jax.nn.sigmoid(a0_ref[...] + _dot(al, a2_ref[...]))
    g = _dot(jax.nn.sigmoid(_dot(mix(5), g1_ref[...])).astype(BF16), g2_ref[...])
    ka = ka_ref[...]
    rk = rk_ref[...]
    a = -kk
    v_o[...] = v
    g_o[...] = g
    a_o[...] = a
    r_o[...] = r
    bonus = jnp.zeros_like(r)
    for dr in range(2):
        z = -wd[:, dr * d:(dr + 1) * d]
        softplus = jnp.maximum(z, 0.0) + jnp.log1p(jnp.exp(-jnp.abs(z)))
        iclr = iclr2[:, dr * d:(dr + 1) * d]
        kd = k * (1.0 + (iclr - 1.0) * ka)
        lw_o[dr] = -jnp.exp(-softplus - 0.5)
        k_o[dr] = kd
        b_o[dr] = kk * iclr
        bonus = bonus + r * kd * rk
    bonus_o[...] = _head_sum(bonus, s1, s2) * v


def _rwkv_proj(st, x, norm_g, mods, vfirst, p, grid_w, tm=256):
    has_vres = vfirst is not None
    tok = st.tok_spec(tm)
    hw = grid_w or CHUNK
    assert tm % hw == 0 and (grid_w or tm == st.l)
    rpt = tm // hw
    last = st.l // hw - 1
    up_spec = pl.BlockSpec((None, hw, st.d), lambda b, i: (b, jnp.maximum(i * rpt - 1, 0), 0))
    down_spec = pl.BlockSpec((None, hw, st.d), lambda b, i: (b, jnp.minimum((i + 1) * rpt, last), 0))
    ins = [x, x, x, norm_g.reshape(1, st.d), mods] + ([vfirst] if has_vres else [])
    specs = [tok, up_spec, down_spec, _const_spec((1, st.d)), st.mod_spec()] + ([tok] if has_vres else [])
    consts = [p['mu'], p['wr'], p['wk'], p['wv'], p['w1'], p['w2'], p['w0'],
              p['a1'], p['a2'], p['a0'], p['g1'], p['g2'], p['kk'], p['ka'], p['rk'],
              p['s1'], p['s2']]
    if has_vres:
        consts += [p['v0'], p['v1'], p['v2']]
    ins += consts
    specs += [_const_spec(c.shape) for c in consts]
    return pl.pallas_call(
        functools.partial(_rwkv_proj_kernel, has_vres, grid_w),
        grid=st.grid(tm),
        in_specs=specs,
        out_specs=[tok] * 5 + [st.dir_spec(tm)] * 3,
        out_shape=[st.shape()] * 5 + [st.dir_shape()] * 3,
        compiler_params=_cparams(_PAR2),
        name="rwkv_proj",
    )(*ins)


def _dot_nt(a, b):
    return lax.dot_general(a, b, (((1,), (1,)), ((), ())), preferred_element_type=F32)


def _split3(x):
    h1 = x.astype(BF16)
    r1 = x - h1.astype(F32)
    h2 = r1.astype(BF16)
    h3 = (r1 - h2.astype(F32)).astype(BF16)
    return h1, h2, h3


def _wkv_chunk_kernel(nb, a_ref, r_ref, v_ref, lw_ref, k_ref, b_ref, s0_ref, m4_ref,
                      y_ref, sf_ref, s_ref):
    dr = pl.program_id(0)
    ch = pl.program_id(1)
    ngroup = a_ref.shape[-1] // GROUP
    rev = dr == 1

    @pl.when(ch == 0)
    def _():
        s_ref[...] = s0_ref[...]

    m4 = m4_ref[...]
    row = lax.broadcasted_iota(jnp.int32, (CHUNK, GROUP), 0)
    col = lax.broadcasted_iota(jnp.int32, (CHUNK, GROUP), 1) % HEAD_DIM
    ahead = jnp.where(rev, row - col, col - row)
    strict = (ahead > 0).astype(F32)
    incl = (ahead >= 0).astype(F32)
    tr = lax.broadcasted_iota(jnp.int32, (CHUNK, CHUNK), 0)
    tc = lax.broadcasted_iota(jnp.int32, (CHUNK, CHUNK), 1)
    tri = (jnp.where(rev, tc - tr, tr - tc) >= 0).astype(F32).astype(BF16)

    def bd(x):
        xb = x.astype(BF16)
        return jnp.concatenate([xb, xb, xb, xb], axis=0) * m4

    def head_t(x):
        t = x.T
        return jnp.concatenate(
            [t[h * HEAD_DIM:(h + 1) * HEAD_DIM, :] for h in range(GROUP // HEAD_DIM)], axis=1)

    def body(it, carry):
        gs = range(ngroup * WKV_NB)
        bis = [it * WKV_NB + g // ngroup for g in gs]
        gi = [g % ngroup for g in gs]
        sl = [slice(gi[g] * GROUP, (gi[g] + 1) * GROUP) for g in gs]
        a = [a_ref[bis[g], :, sl[g]] for g in gs]
        r = [r_ref[bis[g], :, sl[g]] for g in gs]
        v = [v_ref[bis[g], :, sl[g]] for g in gs]
        lw = [lw_ref[bis[g], :, sl[g]] for g in gs]
        k = [k_ref[bis[g], :, sl[g]] for g in gs]
        b = [b_ref[bis[g], :, sl[g]] for g in gs]
        s0 = [s_ref[bis[g], gi[g]] for g in gs]
        cum = []
        for g in gs:
            l1, l2, l3 = _split3(lw[g])
            cum.append(_dot(tri, l1) + _dot(tri, l2) + _dot(tri, l3))
        at = [a[g] * jnp.exp(cum[g] - lw[g]) for g in gs]
        rt = [r[g] * jnp.exp(cum[g]) for g in gs]
        ginv = [jnp.exp(-cum[g]) for g in gs]
        bt = [b[g] * ginv[g] for g in gs]
        kt = [k[g] * ginv[g] for g in gs]
        gtot = [jnp.exp(jnp.where(rev, cum[g][0:1], cum[g][CHUNK - 1:CHUNK])) for g in gs]
        z = [jnp.concatenate([bt[g], kt[g], s0[g]], axis=0).astype(BF16) for g in gs]
        ma = [_dot_nt(z[g], bd(at[g])) for g in gs]
        mr = [_dot_nt(z[g], bd(rt[g])) for g in gs]
        vt = [head_t(v[g]) for g in gs]
        q = [ma[g][0:CHUNK] * strict for g in gs]
        u = [ma[g][2 * CHUNK:] + _dot(vt[g].astype(BF16), bd(ma[g][CHUNK:2 * CHUNK] * strict)) for g in gs]
        for j in range(6):
            for g in gs:
                if j < 5:
                    p = _dot(jnp.concatenate([u[g], q[g]], axis=0).astype(BF16), bd(q[g]))
                    u[g] = u[g] + p[0:CHUNK]
                    q[g] = p[CHUNK:]
                else:
                    u[g] = u[g] + _dot(u[g].astype(BF16), bd(q[g]))
        uv = [jnp.concatenate([u[g], vt[g]], axis=1).astype(BF16) for g in gs]
        yt = [mr[g][2 * CHUNK:] + _dot(uv[g], jnp.concatenate(
            [bd(mr[g][0:CHUNK] * incl), bd(mr[g][CHUNK:2 * CHUNK] * incl)], axis=0)) for g in gs]
        sn = [(s0[g] + _dot(uv[g], jnp.concatenate([bd(bt[g]), bd(kt[g])], axis=0))) * gtot[g]
              for g in gs]
        for g in gs:
            s_ref[bis[g], gi[g]] = sn[g]
            y_ref[bis[g], :, sl[g]] = head_t(yt[g])
        return carry

    lax.fori_loop(0, nb // WKV_NB, body, 0)

    @pl.when(ch == pl.num_programs(1) - 1)
    def _():
        sf_ref[...] = s_ref[...]


def _wkv(st, a, r, v, lw, k, b, s0):
    nb, l, d = st.b, st.l, st.d
    ngroup = d // GROUP
    nch = l // CHUNK
    assert l % CHUNK == 0 and d % GROUP == 0 and nb % WKV_NB == 0

    def chunk(dr, j):
        return jnp.where(dr == 1, nch - 1 - j, j)

    sh_spec = pl.BlockSpec((nb, CHUNK, d), lambda dr, j: (0, chunk(dr, j), 0))
    dir_spec = pl.BlockSpec((None, nb, CHUNK, d), lambda dr, j: (dr, 0, chunk(dr, j), 0))
    st_spec = pl.BlockSpec((None, nb, ngroup, HEAD_DIM, GROUP), lambda dr, j: (dr, 0, 0, 0, 0))
    blk = jnp.arange(GROUP) // HEAD_DIM
    m4 = (blk[:, None] == blk[None, :]).astype(BF16)
    return pl.pallas_call(
        functools.partial(_wkv_chunk_kernel, nb),
        grid=(2, nch),
        in_specs=[sh_spec, sh_spec, sh_spec, dir_spec, dir_spec, dir_spec, st_spec,
                  _const_spec(m4.shape)],
        out_specs=[dir_spec, st_spec],
        out_shape=[jax.ShapeDtypeStruct((2, nb, l, d), F32), jax.ShapeDtypeStruct(s0.shape, F32)],
        scratch_shapes=[pltpu.VMEM((nb, ngroup, HEAD_DIM, GROUP), F32)],
        compiler_params=_cparams(("arbitrary", "arbitrary")),
        name="wkv",
    )(a, r, v, lw, k, b, s0, m4)


def _rwkv_out_kernel(x_ref, y_ref, bonus_ref, g_ref, m_ref, lw_ref, lb_ref, wo_ref,
                     s1_ref, s2_ref, o_ref):
    s1 = s1_ref[...]
    s2 = s2_ref[...]
    y = y_ref[0] + y_ref[1]
    mean = _head_sum(y, s1, s2) * (1.0 / HEAD_DIM)
    yc = y - mean
    var = _head_sum(yc * yc, s1, s2) * (1.0 / HEAD_DIM)
    yn = (yc * lax.rsqrt(var + LNX_EPS)) * lw_ref[...] + lb_ref[...]
    out = (yn + bonus_ref[...]) * g_ref[...]
    o = _dot(out.astype(BF16), wo_ref[...])
    o_ref[...] = x_ref[...] + m_ref[...][2:3] * o


def _rwkv_out(st, x, y, bonus, g, mods, p, tm=256):
    tok = st.tok_spec(tm)
    consts = [p['lnx_w'], p['lnx_b'], p['wo'], p['s1'], p['s2']]
    return pl.pallas_call(
        _rwkv_out_kernel,
        grid=st.grid(tm),
        in_specs=[tok, st.dir_spec(tm), tok, tok, st.mod_spec()]
        + [_const_spec(c.shape) for c in consts],
        out_specs=tok,
        out_shape=st.shape(),
        compiler_params=_cparams(_PAR2),
        name="rwkv_out",
    )(x, y, bonus, g, mods, *consts)


def _mlp_kernel(x_ref, g_ref, m_ref, w1_ref, w2_ref, o_ref):
    x = x_ref[...]
    m = m_ref[...]
    hn = _norm_mod(x, g_ref[...], m[3:4], m[4:5]).astype(BF16)
    d = x.shape[-1]
    f = w1_ref.shape[-1]
    acc = jnp.zeros_like(x)
    for c in range(f // d):
        hid = jnp.maximum(_dot(hn, w1_ref[:, c * d:(c + 1) * d]), 0.0)
        acc = acc + _dot((hid * hid).astype(BF16), w2_ref[c * d:(c + 1) * d, :])
    o_ref[...] = x + m[5:6] * acc


def _mlp(st, x, g, mods, w1, w2, tm=256):
    tm = min(tm, st.l)
    return pl.pallas_call(
        _mlp_kernel,
        grid=st.grid(tm),
        in_specs=[st.tok_spec(tm), _const_spec((1, st.d)), st.mod_spec(),
                  _const_spec(w1.shape), _const_spec(w2.shape)],
        out_specs=st.tok_spec(tm),
        out_shape=st.shape(),
        compiler_params=_cparams(_PAR2),
        name="mlp",
    )(x, g.reshape(1, st.d), mods, w1, w2)


def _dft_mats(n):
    idx = jnp.arange(n, dtype=jnp.int32)
    ang = ((idx[:, None] * idx[None, :]) % n).astype(F32) * (2.0 * jnp.pi / n)
    return jnp.cos(ang), jnp.sin(ang)


def _fnet_chan_kernel(x_ref, g_ref, m_ref, c_ref, s_ref, o_ref):
    m = m_ref[...]
    d = x_ref.shape[-1]
    cg = d // FOURIER_GROUPS
    h = _norm_mod(x_ref[...], g_ref[...], m[0:1], m[1:2]).astype(BF16)
    for gi in range(FOURIER_GROUPS):
        hg = h[:, gi * cg:(gi + 1) * cg]
        o_ref[:, gi * cg:(gi + 1) * cg] = _dot(hg, c_ref[...]).astype(BF16)
        o_ref[:, d + gi * cg:d + (gi + 1) * cg] = _dot(hg, s_ref[...]).astype(BF16)


def _fnet_chan(st, x, g, mods, cc, sc, tm=256):
    return pl.pallas_call(
        _fnet_chan_kernel,
        grid=st.grid(tm),
        in_specs=[st.tok_spec(tm), _const_spec((1, st.d)), st.mod_spec(),
                  _const_spec(cc.shape), _const_spec(sc.shape)],
        out_specs=st.tok_spec(tm, 2 * st.d),
        out_shape=st.shape(2 * st.d, BF16),
        compiler_params=_cparams(_PAR2),
        name="fnet_chan",
    )(x, g.reshape(1, st.d), mods, cc, sc)


def _fnet_seq_kernel(scale, x_ref, xcs_ref, cl_ref, sl_ref, m_ref, wo_ref, o_ref, acc_ref):
    kstep = pl.program_id(2)
    d = x_ref.shape[-1]

    @pl.when(kstep == 0)
    def _():
        acc_ref[...] = jnp.zeros_like(acc_ref)

    acc_ref[...] += _dot(cl_ref[...], xcs_ref[:, :d]) - _dot(sl_ref[...], xcs_ref[:, d:])

    @pl.when(kstep == pl.num_programs(2) - 1)
    def _():
        f = (acc_ref[...] * scale).astype(BF16)
        o_ref[...] = x_ref[...] + m_ref[...][2:3] * _dot(f, wo_ref[...])


def _fnet_seq(st, x, xcs, mods, wo, tl=1024, tk=512):
    d, l = st.d, st.l
    tl = min(tl, l)
    tk = min(tk, l)
    cl, sl = _dft_mats(l)
    cl = cl.astype(BF16)
    sl = sl.astype(BF16)
    scale = 1.0 / float(l * (d // FOURIER_GROUPS)) ** 0.5
    return pl.pallas_call(
        functools.partial(_fnet_seq_kernel, scale),
        grid=(st.b, l // tl, l // tk),
        in_specs=[
            pl.BlockSpec((None, tl, d), lambda b, i, k: (b, i, 0)),
            pl.BlockSpec((None, tk, 2 * d), lambda b, i, k: (b, k, 0)),
            pl.BlockSpec((tl, tk), lambda b, i, k: (i, k)),
            pl.BlockSpec((tl, tk), lambda b, i, k: (i, k)),
            st.mod_spec(),
            _const_spec(wo.shape),
        ],
        out_specs=pl.BlockSpec((None, tl, d), lambda b, i, k: (b, i, 0)),
        out_shape=st.shape(),
        scratch_shapes=[pltpu.VMEM((tl, d), F32)],
        compiler_params=_cparams(("parallel", "parallel", "arbitrary")),
        name="fnet_seq",
    )(x, xcs, cl, sl, mods, wo)


def _final_kernel(x_ref, g_ref, o_ref):
    x = x_ref[...]
    ms = jnp.mean(x * x, axis=-1, keepdims=True)
    o_ref[...] = (x * lax.rsqrt(ms + RMS_EPS)) * g_ref[...]


def _final_norm(st, x, g, tm=256):
    return pl.pallas_call(
        _final_kernel,
        grid=st.grid(tm),
        in_specs=[st.tok_spec(tm), _const_spec((1, st.d))],
        out_specs=st.tok_spec(tm),
        out_shape=st.shape(),
        compiler_params=_cparams(_PAR2),
        name="final_norm",
    )(x, g.reshape(1, st.d))


def _blockdiag2(w):
    z = jnp.zeros_like(w[0])
    return jnp.concatenate([jnp.concatenate([w[0], z], axis=1),
                            jnp.concatenate([z, w[1]], axis=1)], axis=0)


def kernel(x, c, ctx, c_ctx, norm1_g, norm2_g, mod_w, mod_b, mlp_w1, mlp_w2, rk_mu, rk_wr, rk_wk, rk_wv, rk_wo, rk_w0, rk_w1, rk_w2, rk_a0, rk_a1, rk_a2, rk_v0, rk_v1, rk_v2, rk_g1, rk_g2, rk_kk, rk_ka, rk_rk, rk_lnx_w, rk_lnx_b, ft_wo, final_g):
    nb, seq, d = x.shape
    ctx_len = ctx.shape[1]
    depth = mod_w.shape[0]
    assert nb + 1 <= MOD_ROWS and d % LANES == 0
    lat = _Stream(nb, seq, d)
    cst = _Stream(nb, ctx_len, d, ctx_row=nb)
    row = lambda t: t.reshape(1, -1)

    cvec = jnp.zeros((MOD_ROWS, d), F32).at[:nb].set(c).at[nb].set(c_ctx)
    mods_all = _mod_table(cvec, mod_w, mod_b).reshape(depth, MOD_ROWS, N_MOD, d)

    heads = jnp.arange(d) // HEAD_DIM
    s1 = (heads[:, None] == jnp.arange(LANES)[None, :]).astype(BF16)
    s2 = s1.T
    cg = d // FOURIER_GROUPS
    cc, sc = _dft_mats(cg)
    cc = cc.astype(BF16)
    sc = sc.astype(BF16)
    zero_state = jnp.zeros((2, nb, d // GROUP, HEAD_DIM, GROUP), F32)

    last_a = ((depth - 1) // 2) * 2
    vf_lat = vf_ctx = None
    for i in range(depth):
        idx = i // 2
        mods = mods_all[i]
        ctx_in = i <= last_a
        ctx_out = i < last_a
        if i % 2 == 0:
            p = {
                'mu': rk_mu[idx], 'wr': rk_wr[idx].astype(BF16), 'wk': rk_wk[idx].astype(BF16),
                'wv': rk_wv[idx].astype(BF16), 'wo': rk_wo[idx].astype(BF16),
                'w1': jnp.concatenate([rk_w1[idx, 0], rk_w1[idx, 1]], axis=1).astype(BF16),
                'w2': _blockdiag2(rk_w2[idx]).astype(BF16), 'w0': row(rk_w0[idx]),
                'a1': jnp.concatenate([rk_a1[idx, 0], rk_a1[idx, 1]], axis=1).astype(BF16),
                'a2': _blockdiag2(rk_a2[idx]).astype(BF16), 'a0': row(rk_a0[idx]),
                'g1': rk_g1[idx].astype(BF16), 'g2': rk_g2[idx].astype(BF16),
                'kk': row(rk_kk[idx]), 'ka': row(rk_ka[idx]), 'rk': row(rk_rk[idx]),
                'lnx_w': row(rk_lnx_w[idx]), 'lnx_b': row(rk_lnx_b[idx]), 's1': s1, 's2': s2,
            }
            if idx > 0:
                p['v0'] = row(rk_v0[idx - 1])
                p['v1'] = rk_v1[idx - 1].astype(BF16)
                p['v2'] = rk_v2[idx - 1].astype(BF16)
            v_c, g_c, bonus_c, a_c, r_c, lw_c, k_c, b_c = _rwkv_proj(
                cst, ctx, norm1_g[i], mods, vf_ctx if idx > 0 else None, p, 0, tm=ctx_len)
            y_c, s_ctx = _wkv(cst, a_c, r_c, v_c, lw_c, k_c, b_c, zero_state)
            v_l, g_l, bonus_l, a_l, r_l, lw_l, k_l, b_l = _rwkv_proj(
                lat, x, norm1_g[i], mods, vf_lat if idx > 0 else None, p, GRID_W)
            y_l, _ = _wkv(lat, a_l, r_l, v_l, lw_l, k_l, b_l, s_ctx)
            if idx == 0:
                vf_ctx, vf_lat = v_c, v_l
            x = _rwkv_out(lat, x, y_l, bonus_l, g_l, mods, p)
            if ctx_out:
                ctx = _rwkv_out(cst, ctx, y_c, bonus_c, g_c, mods, p)
        else:
            wo = ft_wo[idx].astype(BF16)
            x = _fnet_seq(lat, x, _fnet_chan(lat, x, norm1_g[i], mods, cc, sc), mods, wo)
            if ctx_out:
                ctx = _fnet_seq(cst, ctx, _fnet_chan(cst, ctx, norm1_g[i], mods, cc, sc), mods, wo)
        w1 = mlp_w1[i].astype(BF16)
        w2 = mlp_w2[i].astype(BF16)
        x = _mlp(lat, x, norm2_g[i], mods, w1, w2, tm=512)
        if ctx_out:
            ctx = _mlp(cst, ctx, norm2_g[i], mods, w1, w2)
    return _final_norm(lat, x, final_g)
```

```python
import functools

import jax
import jax.numpy as jnp
from jax import lax
from jax.experimental import pallas as pl
from jax.experimental.pallas import tpu as pltpu

HEAD_DIM = 64
GRID_W = 64
N_MOD = 6
FOURIER_GROUPS = 4
RMS_EPS = 1e-6
LNX_EPS = 64e-5
L2_EPS = 1e-24
DECAY_SCALE = -0.6065306597126334

MOD_ROWS = 16
LANES = 128
GROUP = 256
CHUNK = 64
WKV_NB = 2
FFT_N = 64
FFT_J = 8
VMEM_LIMIT = 56 * 1024 * 1024

F32 = jnp.float32
BF16 = jnp.bfloat16


def _cparams(sem):
    return pltpu.CompilerParams(dimension_semantics=sem, vmem_limit_bytes=VMEM_LIMIT)


def _const_spec(shape):
    nd = len(shape)
    return pl.BlockSpec(shape, lambda *_: (0,) * nd, pipeline_mode=pl.Buffered(1))


def _dot(a, b):
    return jnp.dot(a, b, preferred_element_type=F32)


def _split2(x):
    hi = x.astype(BF16)
    lo = (x - hi.astype(F32)).astype(BF16)
    return hi, lo


def _head_sum(x, s1, s2):
    hi, lo = _split2(x)
    s = _dot(hi, s1) + _dot(lo, s1)
    h1 = s.astype(BF16)
    r1 = s - h1.astype(F32)
    h2 = r1.astype(BF16)
    h3 = (r1 - h2.astype(F32)).astype(BF16)
    return _dot(h1, s2) + _dot(h2, s2) + _dot(h3, s2)


def _norm_mod(x, g, sh, sc):
    ms = jnp.mean(x * x, axis=-1, keepdims=True)
    return (x * lax.rsqrt(ms + RMS_EPS)) * g * (1.0 + sc) + sh


def _mod_kernel(c_ref, w_ref, b_ref, o_ref):
    c = c_ref[...]
    s = c * jax.nn.sigmoid(c)
    o_ref[...] = jnp.dot(s, w_ref[...], precision=lax.Precision.HIGHEST,
                         preferred_element_type=F32) + b_ref[...]


def _mod_table(cvec, mod_w, mod_b):
    depth, d, nd = mod_w.shape
    nt = nd // d
    return pl.pallas_call(
        _mod_kernel,
        grid=(depth, nt),
        in_specs=[
            pl.BlockSpec((MOD_ROWS, d), lambda l, j: (0, 0)),
            pl.BlockSpec((None, d, d), lambda l, j: (l, 0, j)),
            pl.BlockSpec((None, 1, d), lambda l, j: (l, 0, j)),
        ],
        out_specs=pl.BlockSpec((None, MOD_ROWS, d), lambda l, j: (l, 0, j)),
        out_shape=jax.ShapeDtypeStruct((depth, MOD_ROWS, nd), F32),
        compiler_params=_cparams(("arbitrary", "arbitrary")),
        name="mod_table",
    )(cvec, mod_w, mod_b.reshape(depth, 1, nd))


class _Stream:
    def __init__(self, b, l, d, ctx_row=None):
        self.b, self.l, self.d, self.ctx_row = b, l, d, ctx_row

    def grid(self, tm):
        assert self.l % tm == 0
        return (self.b, self.l // tm)

    def tok_spec(self, tm, width=None):
        return pl.BlockSpec((None, tm, width or self.d), lambda b, i: (b, i, 0))

    def dir_spec(self, tm):
        return pl.BlockSpec((2, None, tm, self.d), lambda b, i: (0, b, i, 0))

    def dir_shape(self):
        return jax.ShapeDtypeStruct((2, self.b, self.l, self.d), F32)

    def mod_row(self, b):
        return b if self.ctx_row is None else b * 0 + self.ctx_row

    def mod_spec(self):
        return pl.BlockSpec((None, N_MOD, self.d), lambda b, *_: (self.mod_row(b), 0, 0))

    def shape(self, width=None, dtype=F32):
        return jax.ShapeDtypeStruct((self.b, self.l, width or self.d), dtype)


_PAR2 = ("parallel", "parallel")


def _grid_shift(h, h_up, h_down, grid_w):
    tm, d = h.shape
    q = d // 4
    col = lax.broadcasted_iota(jnp.int32, (tm, q), 0) % grid_w
    left = jnp.where(col == 0, 0.0, pltpu.roll(h[:, 0:q], 1, axis=0))
    right = jnp.where(col == grid_w - 1, 0.0, pltpu.roll(h[:, q:2 * q], tm - 1, axis=0))
    up = jnp.concatenate([h_up[:, 2 * q:3 * q], h[:tm - grid_w, 2 * q:3 * q]], axis=0)
    down = jnp.concatenate([h[grid_w:, 3 * q:], h_down[:, 3 * q:]], axis=0)
    return jnp.concatenate([left, right, up, down], axis=1)


def _line_shift(h):
    tm, d = h.shape
    half = d // 2
    t = lax.broadcasted_iota(jnp.int32, (tm, half), 0)
    prev = jnp.where(t == 0, 0.0, pltpu.roll(h[:, :half], 1, axis=0))
    nxt = jnp.where(t == tm - 1, 0.0, pltpu.roll(h[:, half:], tm - 1, axis=0))
    return jnp.concatenate([prev, nxt], axis=1)


def _rwkv_proj_kernel(has_vres, grid_w, *refs):
    if has_vres:
        (x_ref, xu_ref, xd_ref, ng_ref, m_ref, vf_ref, mu_ref, wr_ref, wk_ref, wv_ref, w1_ref, w2_ref, w0_ref,
         a1_ref, a2_ref, a0_ref, g1_ref, g2_ref, kk_ref, ka_ref, rk_ref, s1_ref, s2_ref,
         v0_ref, v1_ref, v2_ref,
         v_o, g_o, bonus_o, a_o, r_o, lw_o, k_o, b_o) = refs
    else:
        (x_ref, xu_ref, xd_ref, ng_ref, m_ref, mu_ref, wr_ref, wk_ref, wv_ref, w1_ref, w2_ref, w0_ref,
         a1_ref, a2_ref, a0_ref, g1_ref, g2_ref, kk_ref, ka_ref, rk_ref, s1_ref, s2_ref,
         v_o, g_o, bonus_o, a_o, r_o, lw_o, k_o, b_o) = refs
    d = x_ref.shape[-1]
    m = m_ref[...]
    ng = ng_ref[...]
    h = _norm_mod(x_ref[...], ng, m[0:1], m[1:2])
    if grid_w:
        i = pl.program_id(1)
        h_up = jnp.where(i > 0, _norm_mod(xu_ref[...], ng, m[0:1], m[1:2]), 0.0)
        h_down = jnp.where(i < pl.num_programs(1) - 1, _norm_mod(xd_ref[...], ng, m[0:1], m[1:2]), 0.0)
        xx = _grid_shift(h, h_up, h_down, grid_w) - h
    else:
        xx = _line_shift(h) - h
    mu = mu_ref[...]
    s1 = s1_ref[...]
    s2 = s2_ref[...]

    def mix(j):
        return (h + xx * mu[j:j + 1]).astype(BF16)

    r = _dot(mix(0), wr_ref[...])
    k = _dot(mix(2), wk_ref[...])
    xv = mix(3)
    v = _dot(xv, wv_ref[...])
    if has_vres:
        lo = _dot(_dot(xv, v1_ref[...]).astype(BF16), v2_ref[...])
        v = v + (vf_ref[...] - v) * jax.nn.sigmoid(v0_ref[...] + lo)
    kk = k * kk_ref[...]
    kk = kk * lax.rsqrt(jnp.maximum(_head_sum(kk * kk, s1, s2), L2_EPS))
    wl = jnp.tanh(_dot(mix(1), w1_ref[...])).astype(BF16)
    wd = w0_ref[...] + _dot(wl, w2_ref[...])
    al = _dot(mix(4), a1_ref[...]).astype(BF16)
    iclr2 = jax.nn.sigmoid(a0_ref[...] + _dot(al, a2_ref[...]))
    g = _dot(jax.nn.sigmoid(_dot(mix(5), g1_ref[...])).astype(BF16), g2_ref[...])
    ka = ka_ref[...]
    rk = rk_ref[...]
    a = -kk
    v_o[...] = v
    g_o[...] = g
    a_o[...] = a
    r_o[...] = r
    bonus = jnp.zeros_like(r)
    for dr in range(2):
        iclr = iclr2[:, dr * d:(dr + 1) * d]
        kd = k * (1.0 + (iclr - 1.0) * ka)
        lw_o[dr] = jax.nn.sigmoid(wd[:, dr * d:(dr + 1) * d]) * DECAY_SCALE
        k_o[dr] = kd
        b_o[dr] = kk * iclr
        bonus = bonus + r * kd * rk
    bonus_o[...] = _head_sum(bonus, s1, s2) * v


def _rwkv_proj(st, x, norm_g, mods, vfirst, p, grid_w, tm=256):
    has_vres = vfirst is not None
    tok = st.tok_spec(tm)
    hw = grid_w or CHUNK
    assert tm % hw == 0 and (grid_w or tm == st.l)
    rpt = tm // hw
    last = st.l // hw - 1
    up_spec = pl.BlockSpec((None, hw, st.d), lambda b, i: (b, jnp.maximum(i * rpt - 1, 0), 0))
    down_spec = pl.BlockSpec((None, hw, st.d), lambda b, i: (b, jnp.minimum((i + 1) * rpt, last), 0))
    ins = [x, x, x, norm_g.reshape(1, st.d), mods] + ([vfirst] if has_vres else [])
    specs = [tok, up_spec, down_spec, _const_spec((1, st.d)), st.mod_spec()] + ([tok] if has_vres else [])
    consts = [p['mu'], p['wr'], p['wk'], p['wv'], p['w1'], p['w2'], p['w0'],
              p['a1'], p['a2'], p['a0'], p['g1'], p['g2'], p['kk'], p['ka'], p['rk'],
              p['s1'], p['s2']]
    if has_vres:
        consts += [p['v0'], p['v1'], p['v2']]
    ins += consts
    specs += [_const_spec(c.shape) for c in consts]
    return pl.pallas_call(
        functools.partial(_rwkv_proj_kernel, has_vres, grid_w),
        grid=st.grid(tm),
        in_specs=specs,
        out_specs=[tok] * 5 + [st.dir_spec(tm)] * 3,
        out_shape=[st.shape()] * 5 + [st.dir_shape()] * 3,
        compiler_params=_cparams(_PAR2),
        name="rwkv_proj",
    )(*ins)


def _dot_nt(a, b):
    return lax.dot_general(a, b, (((1,), (1,)), ((), ())), preferred_element_type=F32)


def _split3(x):
    h1 = x.astype(BF16)
    r1 = x - h1.astype(F32)
    h2 = r1.astype(BF16)
    h3 = (r1 - h2.astype(F32)).astype(BF16)
    return h1, h2, h3


def _wkv_chunk_kernel(nb, a_ref, r_ref, v_ref, lw_ref, k_ref, b_ref, s0_ref, m4_ref,
                      y_ref, sf_ref, s_ref):
    dr = pl.program_id(0)
    ch = pl.program_id(1)
    ngroup = a_ref.shape[-1] // GROUP
    rev = dr == 1

    @pl.when(ch == 0)
    def _():
        s_ref[...] = s0_ref[...]

    m4 = m4_ref[...]
    row = lax.broadcasted_iota(jnp.int32, (CHUNK, GROUP), 0)
    col = lax.broadcasted_iota(jnp.int32, (CHUNK, GROUP), 1) % HEAD_DIM
    ahead = jnp.where(rev, row - col, col - row)
    strict = (ahead > 0).astype(F32)
    incl = (ahead >= 0).astype(F32)
    tr = lax.broadcasted_iota(jnp.int32, (CHUNK, CHUNK), 0)
    tc = lax.broadcasted_iota(jnp.int32, (CHUNK, CHUNK), 1)
    tri = (jnp.where(rev, tc - tr, tr - tc) >= 0).astype(F32).astype(BF16)

    def bd(x):
        xb = x.astype(BF16)
        return jnp.concatenate([xb, xb, xb, xb], axis=0) * m4

    def head_t(x):
        t = x.T
        return jnp.concatenate(
            [t[h * HEAD_DIM:(h + 1) * HEAD_DIM, :] for h in range(GROUP // HEAD_DIM)], axis=1)

    def body(it, carry):
        gs = range(ngroup * WKV_NB)
        bis = [it * WKV_NB + g // ngroup for g in gs]
        gi = [g % ngroup for g in gs]
        sl = [slice(gi[g] * GROUP, (gi[g] + 1) * GROUP) for g in gs]
        a = [a_ref[bis[g], :, sl[g]] for g in gs]
        r = [r_ref[bis[g], :, sl[g]] for g in gs]
        v = [v_ref[bis[g], :, sl[g]] for g in gs]
        lw = [lw_ref[bis[g], :, sl[g]] for g in gs]
        k = [k_ref[bis[g], :, sl[g]] for g in gs]
        b = [b_ref[bis[g], :, sl[g]] for g in gs]
        s0 = [s_ref[bis[g], gi[g]] for g in gs]
        cum = []
        for g in gs:
            l1, l2, l3 = _split3(lw[g])
            cum.append(_dot(tri, l1) + _dot(tri, l2) + _dot(tri, l3))
        at = [a[g] * jnp.exp(cum[g] - lw[g]) for g in gs]
        rt = [r[g] * jnp.exp(cum[g]) for g in gs]
        ginv = [jnp.exp(-cum[g]) for g in gs]
        bt = [b[g] * ginv[g] for g in gs]
        kt = [k[g] * ginv[g] for g in gs]
        gtot = [jnp.exp(jnp.where(rev, cum[g][0:1], cum[g][CHUNK - 1:CHUNK])) for g in gs]
        z = [jnp.concatenate([bt[g], kt[g], s0[g]], axis=0).astype(BF16) for g in gs]
        ma = [_dot_nt(z[g], bd(at[g])) for g in gs]
        mr = [_dot_nt(z[g], bd(rt[g])) for g in gs]
        vt = [head_t(v[g]) for g in gs]
        q = [ma[g][0:CHUNK] * strict for g in gs]
        u = [ma[g][2 * CHUNK:] + _dot(vt[g].astype(BF16), bd(ma[g][CHUNK:2 * CHUNK] * strict)) for g in gs]
        for j in range(6):
            for g in gs:
                if j < 5:
                    p = _dot(jnp.concatenate([u[g], q[g]], axis=0).astype(BF16), bd(q[g]))
                    u[g] = u[g] + p[0:CHUNK]
                    q[g] = p[CHUNK:]
                else:
                    u[g] = u[g] + _dot(u[g].astype(BF16), bd(q[g]))
        uv = [jnp.concatenate([u[g], vt[g]], axis=1).astype(BF16) for g in gs]
        yt = [mr[g][2 * CHUNK:] + _dot(uv[g], jnp.concatenate(
            [bd(mr[g][0:CHUNK] * incl), bd(mr[g][CHUNK:2 * CHUNK] * incl)], axis=0)) for g in gs]
        sn = [(s0[g] + _dot(uv[g], jnp.concatenate([bd(bt[g]), bd(kt[g])], axis=0))) * gtot[g]
              for g in gs]
        for g in gs:
            s_ref[bis[g], gi[g]] = sn[g]
            y_ref[bis[g], :, sl[g]] = head_t(yt[g])
        return carry

    lax.fori_loop(0, nb // WKV_NB, body, 0)

    @pl.when(ch == pl.num_programs(1) - 1)
    def _():
        sf_ref[...] = s_ref[...]


def _wkv(st, a, r, v, lw, k, b, s0):
    nb, l, d = st.b, st.l, st.d
    ngroup = d // GROUP
    nch = l // CHUNK
    assert l % CHUNK == 0 and d % GROUP == 0 and nb % WKV_NB == 0

    def chunk(dr, j):
        return jnp.where(dr == 1, nch - 1 - j, j)

    sh_spec = pl.BlockSpec((nb, CHUNK, d), lambda dr, j: (0, chunk(dr, j), 0))
    dir_spec = pl.BlockSpec((None, nb, CHUNK, d), lambda dr, j: (dr, 0, chunk(dr, j), 0))
    st_spec = pl.BlockSpec((None, nb, ngroup, HEAD_DIM, GROUP), lambda dr, j: (dr, 0, 0, 0, 0))
    blk = jnp.arange(GROUP) // HEAD_DIM
    m4 = (blk[:, None] == blk[None, :]).astype(BF16)
    return pl.pallas_call(
        functools.partial(_wkv_chunk_kernel, nb),
        grid=(2, nch),
        in_specs=[sh_spec, sh_spec, sh_spec, dir_spec, dir_spec, dir_spec, st_spec,
                  _const_spec(m4.shape)],
        out_specs=[dir_spec, st_spec],
        out_shape=[jax.ShapeDtypeStruct((2, nb, l, d), F32), jax.ShapeDtypeStruct(s0.shape, F32)],
        scratch_shapes=[pltpu.VMEM((nb, ngroup, HEAD_DIM, GROUP), F32)],
        compiler_params=_cparams(("arbitrary", "arbitrary")),
        name="wkv",
    )(a, r, v, lw, k, b, s0, m4)


def _rwkv_out_kernel(x_ref, y_ref, bonus_ref, g_ref, m_ref, lw_ref, lb_ref, wo_ref,
                     s1_ref, s2_ref, o_ref):
    s1 = s1_ref[...]
    s2 = s2_ref[...]
    y = y_ref[0] + y_ref[1]
    mean = _head_sum(y, s1, s2) * (1.0 / HEAD_DIM)
    yc = y - mean
    var = _head_sum(yc * yc, s1, s2) * (1.0 / HEAD_DIM)
    yn = (yc * lax.rsqrt(var + LNX_EPS)) * lw_ref[...] + lb_ref[...]
    out = (yn + bonus_ref[...]) * g_ref[...]
    o = _dot(out.astype(BF16), wo_ref[...])
    o_ref[...] = x_ref[...] + m_ref[...][2:3] * o


def _rwkv_out(st, x, y, bonus, g, mods, p, tm=256):
    tok = st.tok_spec(tm)
    consts = [p['lnx_w'], p['lnx_b'], p['wo'], p['s1'], p['s2']]
    return pl.pallas_call(
        _rwkv_out_kernel,
        grid=st.grid(tm),
        in_specs=[tok, st.dir_spec(tm), tok, tok, st.mod_spec()]
        + [_const_spec(c.shape) for c in consts],
        out_specs=tok,
        out_shape=st.shape(),
        compiler_params=_cparams(_PAR2),
        name="rwkv_out",
    )(x, y, bonus, g, mods, *consts)


def _mlp_kernel(x_ref, g_ref, m_ref, w1_ref, w2_ref, o_ref):
    x = x_ref[...]
    m = m_ref[...]
    hn = _norm_mod(x, g_ref[...], m[3:4], m[4:5]).astype(BF16)
    d = x.shape[-1]
    f = w1_ref.shape[-1]
    acc = jnp.zeros_like(x)
    for c in range(f // d):
        hid = jnp.maximum(_dot(hn, w1_ref[:, c * d:(c + 1) * d]), 0.0)
        acc = acc + _dot((hid * hid).astype(BF16), w2_ref[c * d:(c + 1) * d, :])
    o_ref[...] = x + m[5:6] * acc


def _mlp(st, x, g, mods, w1, w2, tm=256):
    tm = min(tm, st.l)
    return pl.pallas_call(
        _mlp_kernel,
        grid=st.grid(tm),
        in_specs=[st.tok_spec(tm), _const_spec((1, st.d)), st.mod_spec(),
                  _const_spec(w1.shape), _const_spec(w2.shape)],
        out_specs=st.tok_spec(tm),
        out_shape=st.shape(),
        compiler_params=_cparams(_PAR2),
        name="mlp",
    )(x, g.reshape(1, st.d), mods, w1, w2)


def _dft_mats(n):
    idx = jnp.arange(n, dtype=jnp.int32)
    ang = ((idx[:, None] * idx[None, :]) % n).astype(F32) * (2.0 * jnp.pi / n)
    return jnp.cos(ang), jnp.sin(ang)


def _fnet_chan_kernel(x_ref, g_ref, m_ref, c_ref, s_ref, o_ref):
    m = m_ref[...]
    d = x_ref.shape[-1]
    cg = d // FOURIER_GROUPS
    h = _norm_mod(x_ref[...], g_ref[...], m[0:1], m[1:2]).astype(BF16)
    for gi in range(FOURIER_GROUPS):
        hg = h[:, gi * cg:(gi + 1) * cg]
        o_ref[:, gi * cg:(gi + 1) * cg] = _dot(hg, c_ref[...]).astype(BF16)
        o_ref[:, d + gi * cg:d + (gi + 1) * cg] = _dot(hg, s_ref[...]).astype(BF16)


def _fnet_chan(st, x, g, mods, cc, sc, tm=256):
    return pl.pallas_call(
        _fnet_chan_kernel,
        grid=st.grid(tm),
        in_specs=[st.tok_spec(tm), _const_spec((1, st.d)), st.mod_spec(),
                  _const_spec(cc.shape), _const_spec(sc.shape)],
        out_specs=st.tok_spec(tm, 2 * st.d),
        out_shape=st.shape(2 * st.d, BF16),
        compiler_params=_cparams(_PAR2),
        name="fnet_chan",
    )(x, g.reshape(1, st.d), mods, cc, sc)


def _fnet_seq_kernel(scale, x_ref, xcs_ref, cl_ref, sl_ref, m_ref, wo_ref, o_ref, acc_ref):
    kstep = pl.program_id(2)
    d = x_ref.shape[-1]

    @pl.when(kstep == 0)
    def _():
        acc_ref[...] = jnp.zeros_like(acc_ref)

    acc_ref[...] += _dot(cl_ref[...], xcs_ref[:, :d]) - _dot(sl_ref[...], xcs_ref[:, d:])

    @pl.when(kstep == pl.num_programs(2) - 1)
    def _():
        f = (acc_ref[...] * scale).astype(BF16)
        o_ref[...] = x_ref[...] + m_ref[...][2:3] * _dot(f, wo_ref[...])


def _fnet_seq(st, x, xcs, mods, wo, tl=1024, tk=512):
    d, l = st.d, st.l
    tl = min(tl, l)
    tk = min(tk, l)
    cl, sl = _dft_mats(l)
    cl = cl.astype(BF16)
    sl = sl.astype(BF16)
    scale = 1.0 / float(l * (d // FOURIER_GROUPS)) ** 0.5
    return pl.pallas_call(
        functools.partial(_fnet_seq_kernel, scale),
        grid=(st.b, l // tl, l // tk),
        in_specs=[
            pl.BlockSpec((None, tl, d), lambda b, i, k: (b, i, 0)),
            pl.BlockSpec((None, tk, 2 * d), lambda b, i, k: (b, k, 0)),
            pl.BlockSpec((tl, tk), lambda b, i, k: (i, k)),
            pl.BlockSpec((tl, tk), lambda b, i, k: (i, k)),
            st.mod_spec(),
            _const_spec(wo.shape),
        ],
        out_specs=pl.BlockSpec((None, tl, d), lambda b, i, k: (b, i, 0)),
        out_shape=st.shape(),
        scratch_shapes=[pltpu.VMEM((tl, d), F32)],
        compiler_params=_cparams(("parallel", "parallel", "arbitrary")),
        name="fnet_seq",
    )(x, xcs, cl, sl, mods, wo)


def _fft1_kernel(x_ref, g_ref, m_ref, cc_ref, sc_ref, w1_ref, tc_ref, ts_ref, o_ref):
    n, j_n, d = x_ref.shape
    cg = d // FOURIER_GROUPS
    m = m_ref[...]
    xs = jnp.concatenate([x_ref[:, j, :] for j in range(j_n)], axis=0)
    h = _norm_mod(xs, g_ref[...], m[0:1], m[1:2]).astype(BF16)
    xc = []
    xsn = []
    for gi in range(FOURIER_GROUPS):
        hg = h[:, gi * cg:(gi + 1) * cg]
        xc.append(_dot(hg, cc_ref[...]).astype(BF16))
        xsn.append(_dot(hg, sc_ref[...]).astype(BF16))
    xcs = jnp.concatenate(xc + xsn, axis=1)
    w1 = w1_ref[...]
    for j in range(j_n):
        p = _dot(w1, xcs[j * n:(j + 1) * n])
        a_re = p[:n, :d] - p[n:, d:]
        a_im = -(p[:n, d:] + p[n:, :d])
        tc = jnp.concatenate([tc_ref[j]] * (d // LANES), axis=1)
        ts = jnp.concatenate([ts_ref[j]] * (d // LANES), axis=1)
        o_ref[:, j, :d] = (a_re * tc + a_im * ts).astype(BF16)
        o_ref[:, j, d:] = (a_im * tc - a_re * ts).astype(BF16)


def _fft2_kernel(scale, x_ref, b_ref, m_ref, w2_ref, wo_ref, o_ref):
    n, j_n, d = x_ref.shape
    w2 = w2_ref[...]
    f = []
    for j in range(j_n):
        bj = b_ref[j]
        f.append(_dot(w2, jnp.concatenate([bj[:, :d], bj[:, d:]], axis=0)))
    fa = (jnp.concatenate(f, axis=0) * scale).astype(BF16)
    o = _dot(fa, wo_ref[...])
    ga = m_ref[...][2:3]
    for j in range(j_n):
        o_ref[:, j, :] = x_ref[:, j, :] + ga * o[j * n:(j + 1) * n]


def _fnet_fft(st, x, g, mods, cc, sc, wo):
    nb, l, d = st.b, st.l, st.d
    n, jb = FFT_N, FFT_J
    assert l == n * n
    c1, s1 = _dft_mats(n)
    w1 = jnp.concatenate([c1, s1], axis=0).astype(BF16)
    w2 = jnp.concatenate([c1, s1], axis=1).astype(BF16)
    idx = jnp.arange(n, dtype=jnp.int32)
    ang = (idx[:, None] * idx[None, :]).astype(F32) * (2.0 * jnp.pi / l)
    tc = jnp.broadcast_to(jnp.cos(ang)[:, :, None], (n, n, LANES))
    ts = jnp.broadcast_to(jnp.sin(ang)[:, :, None], (n, n, LANES))
    scale = 1.0 / float(l * (d // FOURIER_GROUPS)) ** 0.5
    x4 = x.reshape(nb, n, n, d)
    blk = pl.BlockSpec((None, n, jb, d), lambda b, i: (b, 0, i, 0))
    tw = pl.BlockSpec((jb, n, LANES), lambda b, i: (i, 0, 0))
    bmat = pl.pallas_call(
        _fft1_kernel,
        grid=(nb, n // jb),
        in_specs=[blk, _const_spec((1, d)), st.mod_spec(), _const_spec(cc.shape), _const_spec(sc.shape),
                  _const_spec(w1.shape), tw, tw],
        out_specs=pl.BlockSpec((None, n, jb, 2 * d), lambda b, i: (b, 0, i, 0)),
        out_shape=jax.ShapeDtypeStruct((nb, n, n, 2 * d), BF16),
        compiler_params=_cparams(_PAR2),
        name="fnet_fft1",
    )(x4, g.reshape(1, d), mods, cc, sc, w1, tc, ts)
    out = pl.pallas_call(
        functools.partial(_fft2_kernel, scale),
        grid=(nb, n // jb),
        in_specs=[blk, pl.BlockSpec((None, jb, n, 2 * d), lambda b, i: (b, i, 0, 0)),
                  st.mod_spec(), _const_spec(w2.shape), _const_spec(wo.shape)],
        out_specs=blk,
        out_shape=jax.ShapeDtypeStruct((nb, n, n, d), F32),
        compiler_params=_cparams(_PAR2),
        name="fnet_fft2",
    )(x4, bmat, mods, w2, wo)
    return out.reshape(nb, l, d)


def _final_kernel(x_ref, g_ref, o_ref):
    x = x_ref[...]
    ms = jnp.mean(x * x, axis=-1, keepdims=True)
    o_ref[...] = (x * lax.rsqrt(ms + RMS_EPS)) * g_ref[...]


def _final_norm(st, x, g, tm=256):
    return pl.pallas_call(
        _final_kernel,
        grid=st.grid(tm),
        in_specs=[st.tok_spec(tm), _const_spec((1, st.d))],
        out_specs=st.tok_spec(tm),
        out_shape=st.shape(),
        compiler_params=_cparams(_PAR2),
        name="final_norm",
    )(x, g.reshape(1, st.d))


def _blockdiag2(w):
    z = jnp.zeros_like(w[0])
    return jnp.concatenate([jnp.concatenate([w[0], z], axis=1),
                            jnp.concatenate([z, w[1]], axis=1)], axis=0)


def kernel(x, c, ctx, c_ctx, norm1_g, norm2_g, mod_w, mod_b, mlp_w1, mlp_w2, rk_mu, rk_wr, rk_wk, rk_wv, rk_wo, rk_w0, rk_w1, rk_w2, rk_a0, rk_a1, rk_a2, rk_v0, rk_v1, rk_v2, rk_g1, rk_g2, rk_kk, rk_ka, rk_rk, rk_lnx_w, rk_lnx_b, ft_wo, final_g):
    nb, seq, d = x.shape
    ctx_len = ctx.shape[1]
    depth = mod_w.shape[0]
    assert nb + 1 <= MOD_ROWS and d % LANES == 0
    lat = _Stream(nb, seq, d)
    cst = _Stream(nb, ctx_len, d, ctx_row=nb)
    row = lambda t: t.reshape(1, -1)

    cvec = jnp.zeros((MOD_ROWS, d), F32).at[:nb].set(c).at[nb].set(c_ctx)
    mods_all = _mod_table(cvec, mod_w, mod_b).reshape(depth, MOD_ROWS, N_MOD, d)

    heads = jnp.arange(d) // HEAD_DIM
    s1 = (heads[:, None] == jnp.arange(LANES)[None, :]).astype(BF16)
    s2 = s1.T
    cg = d // FOURIER_GROUPS
    cc, sc = _dft_mats(cg)
    cc = cc.astype(BF16)
    sc = sc.astype(BF16)
    zero_state = jnp.zeros((2, nb, d // GROUP, HEAD_DIM, GROUP), F32)

    last_a = ((depth - 1) // 2) * 2
    vf_lat = vf_ctx = None
    for i in range(depth):
        idx = i // 2
        mods = mods_all[i]
        ctx_in = i <= last_a
        ctx_out = i < last_a
        if i % 2 == 0:
            p = {
                'mu': rk_mu[idx], 'wr': rk_wr[idx].astype(BF16), 'wk': rk_wk[idx].astype(BF16),
                'wv': rk_wv[idx].astype(BF16), 'wo': rk_wo[idx].astype(BF16),
                'w1': jnp.concatenate([rk_w1[idx, 0], rk_w1[idx, 1]], axis=1).astype(BF16),
                'w2': _blockdiag2(rk_w2[idx]).astype(BF16), 'w0': row(rk_w0[idx]),
                'a1': jnp.concatenate([rk_a1[idx, 0], rk_a1[idx, 1]], axis=1).astype(BF16),
                'a2': _blockdiag2(rk_a2[idx]).astype(BF16), 'a0': row(rk_a0[idx]),
                'g1': rk_g1[idx].astype(BF16), 'g2': rk_g2[idx].astype(BF16),
                'kk': row(rk_kk[idx]), 'ka': row(rk_ka[idx]), 'rk': row(rk_rk[idx]),
                'lnx_w': row(rk_lnx_w[idx]), 'lnx_b': row(rk_lnx_b[idx]), 's1': s1, 's2': s2,
            }
            if idx > 0:
                p['v0'] = row(rk_v0[idx - 1])
                p['v1'] = rk_v1[idx - 1].astype(BF16)
                p['v2'] = rk_v2[idx - 1].astype(BF16)
            v_c, g_c, bonus_c, a_c, r_c, lw_c, k_c, b_c = _rwkv_proj(
                cst, ctx, norm1_g[i], mods, vf_ctx if idx > 0 else None, p, 0, tm=ctx_len)
            y_c, s_ctx = _wkv(cst, a_c, r_c, v_c, lw_c, k_c, b_c, zero_state)
            v_l, g_l, bonus_l, a_l, r_l, lw_l, k_l, b_l = _rwkv_proj(
                lat, x, norm1_g[i], mods, vf_lat if idx > 0 else None, p, GRID_W)
            y_l, _ = _wkv(lat, a_l, r_l, v_l, lw_l, k_l, b_l, s_ctx)
            if idx == 0:
                vf_ctx, vf_lat = v_c, v_l
            x = _rwkv_out(lat, x, y_l, bonus_l, g_l, mods, p)
            if ctx_out:
                ctx = _rwkv_out(cst, ctx, y_c, bonus_c, g_c, mods, p)
        else:
            wo = ft_wo[idx].astype(BF16)
            if seq == FFT_N * FFT_N:
                x = _fnet_fft(lat, x, norm1_g[i], mods, cc, sc, wo)
            else:
                x = _fnet_seq(lat, x, _fnet_chan(lat, x, norm1_g[i], mods, cc, sc), mods, wo)
            if ctx_out:
                ctx = _fnet_seq(cst, ctx, _fnet_chan(cst, ctx, norm1_g[i], mods, cc, sc), mods, wo)
        w1 = mlp_w1[i].astype(BF16)
        w2 = mlp_w2[i].astype(BF16)
        x = _mlp(lat, x, norm2_g[i], mods, w1, w2, tm=512)
        if ctx_out:
            ctx = _mlp(cst, ctx, norm2_g[i], mods, w1, w2)
    return _final_norm(lat, x, final_g)
```

```python
import functools

import jax
import jax.numpy as jnp
from jax import lax
from jax.experimental import pallas as pl
from jax.experimental.pallas import tpu as pltpu

HEAD_DIM = 64
GRID_W = 64
N_MOD = 6
FOURIER_GROUPS = 4
RMS_EPS = 1e-6
LNX_EPS = 64e-5
L2_EPS = 1e-24
DECAY_SCALE = -0.6065306597126334

MOD_ROWS = 16
LANES = 128
GROUP = 256
CHUNK = 64
WKV_NB = 4
FFT_N = 64
FFT_J = 8
VMEM_LIMIT = 56 * 1024 * 1024

F32 = jnp.float32
BF16 = jnp.bfloat16


def _cparams(sem):
    return pltpu.CompilerParams(dimension_semantics=sem, vmem_limit_bytes=VMEM_LIMIT)


def _const_spec(shape):
    nd = len(shape)
    return pl.BlockSpec(shape, lambda *_: (0,) * nd, pipeline_mode=pl.Buffered(1))


def _dot(a, b):
    return jnp.dot(a, b, preferred_element_type=F32)


def _split2(x):
    hi = x.astype(BF16)
    lo = (x - hi.astype(F32)).astype(BF16)
    return hi, lo


def _head_sum(x, s1, s2):
    s = _dot(x.astype(BF16), s1)
    hi, lo = _split2(s)
    return _dot(hi, s2) + _dot(lo, s2)


def _norm_mod(x, g, sh, sc):
    ms = jnp.mean(x * x, axis=-1, keepdims=True)
    return (x * lax.rsqrt(ms + RMS_EPS)) * g * (1.0 + sc) + sh


def _mod_kernel(c_ref, w_ref, b_ref, o_ref):
    c = c_ref[...]
    s = c * jax.nn.sigmoid(c)
    o_ref[...] = jnp.dot(s, w_ref[...], precision=lax.Precision.HIGHEST,
                         preferred_element_type=F32) + b_ref[...]


def _mod_table(cvec, mod_w, mod_b):
    depth, d, nd = mod_w.shape
    nt = nd // d
    return pl.pallas_call(
        _mod_kernel,
        grid=(depth, nt),
        in_specs=[
            pl.BlockSpec((MOD_ROWS, d), lambda l, j: (0, 0)),
            pl.BlockSpec((None, d, d), lambda l, j: (l, 0, j)),
            pl.BlockSpec((None, 1, d), lambda l, j: (l, 0, j)),
        ],
        out_specs=pl.BlockSpec((None, MOD_ROWS, d), lambda l, j: (l, 0, j)),
        out_shape=jax.ShapeDtypeStruct((depth, MOD_ROWS, nd), F32),
        compiler_params=_cparams(("arbitrary", "arbitrary")),
        name="mod_table",
    )(cvec, mod_w, mod_b.reshape(depth, 1, nd))


class _Stream:
    def __init__(self, b, l, d, ctx_row=None):
        self.b, self.l, self.d, self.ctx_row = b, l, d, ctx_row

    def grid(self, tm):
        assert self.l % tm == 0
        return (self.b, self.l // tm)

    def tok_spec(self, tm, width=None):
        return pl.BlockSpec((None, tm, width or self.d), lambda b, i: (b, i, 0))

    def dir_spec(self, tm):
        return pl.BlockSpec((2, None, tm, self.d), lambda b, i: (0, b, i, 0))

    def dir_shape(self):
        return jax.ShapeDtypeStruct((2, self.b, self.l, self.d), F32)

    def mod_row(self, b):
        return b if self.ctx_row is None else b * 0 + self.ctx_row

    def mod_spec(self):
        return pl.BlockSpec((None, N_MOD, self.d), lambda b, *_: (self.mod_row(b), 0, 0))

    def shape(self, width=None, dtype=F32):
        return jax.ShapeDtypeStruct((self.b, self.l, width or self.d), dtype)


_PAR2 = ("parallel", "parallel")


def _grid_shift(h, h_up, h_down, grid_w):
    tm, d = h.shape
    q = d // 4
    col = lax.broadcasted_iota(jnp.int32, (tm, q), 0) % grid_w
    left = jnp.where(col == 0, 0.0, pltpu.roll(h[:, 0:q], 1, axis=0))
    right = jnp.where(col == grid_w - 1, 0.0, pltpu.roll(h[:, q:2 * q], tm - 1, axis=0))
    up = jnp.concatenate([h_up[:, 2 * q:3 * q], h[:tm - grid_w, 2 * q:3 * q]], axis=0)
    down = jnp.concatenate([h[grid_w:, 3 * q:], h_down[:, 3 * q:]], axis=0)
    return jnp.concatenate([left, right, up, down], axis=1)


def _line_shift(h):
    tm, d = h.shape
    half = d // 2
    t = lax.broadcasted_iota(jnp.int32, (tm, half), 0)
    prev = jnp.where(t == 0, 0.0, pltpu.roll(h[:, :half], 1, axis=0))
    nxt = jnp.where(t == tm - 1, 0.0, pltpu.roll(h[:, half:], tm - 1, axis=0))
    return jnp.concatenate([prev, nxt], axis=1)


def _rwkv_proj_kernel(has_vres, grid_w, *refs):
    if has_vres:
        (x_ref, xu_ref, xd_ref, ng_ref, m_ref, vf_ref, mu_ref, wr_ref, wk_ref, wv_ref, w1_ref, w2_ref, w0_ref,
         a1_ref, a2_ref, a0_ref, g1_ref, g2_ref, kk_ref, ka_ref, rk_ref, s1_ref, s2_ref,
         v0_ref, v1_ref, v2_ref,
         v_o, g_o, bonus_o, a_o, r_o, lw_o, k_o, b_o) = refs
    else:
        (x_ref, xu_ref, xd_ref, ng_ref, m_ref, mu_ref, wr_ref, wk_ref, wv_ref, w1_ref, w2_ref, w0_ref,
         a1_ref, a2_ref, a0_ref, g1_ref, g2_ref, kk_ref, ka_ref, rk_ref, s1_ref, s2_ref,
         v_o, g_o, bonus_o, a_o, r_o, lw_o, k_o, b_o) = refs
    d = x_ref.shape[-1]
    m = m_ref[...]
    ng = ng_ref[...]
    h = _norm_mod(x_ref[...], ng, m[0:1], m[1:2])
    if grid_w:
        i = pl.program_id(1)
        h_up = jnp.where(i > 0, _norm_mod(xu_ref[...], ng, m[0:1], m[1:2]), 0.0)
        h_down = jnp.where(i < pl.num_programs(1) - 1, _norm_mod(xd_ref[...], ng, m[0:1], m[1:2]), 0.0)
        xx = _grid_shift(h, h_up, h_down, grid_w) - h
    else:
        xx = _line_shift(h) - h
    mu = mu_ref[...]
    s1 = s1_ref[...]
    s2 = s2_ref[...]

    def mix(j):
        return (h + xx * mu[j:j + 1]).astype(BF16)

    r = _dot(mix(0), wr_ref[...])
    k = _dot(mix(2), wk_ref[...])
    xv = mix(3)
    v = _dot(xv, wv_ref[...])
    if has_vres:
        lo = _dot(_dot(xv, v1_ref[...]).astype(BF16), v2_ref[...])
        v = v + (vf_ref[...] - v) * jax.nn.sigmoid(v0_ref[...] + lo)
    kk = k * kk_ref[...]
    kk = kk * lax.rsqrt(jnp.maximum(_head_sum(kk * kk, s1, s2), L2_EPS))
    wl = jnp.tanh(_dot(mix(1), w1_ref[...])).astype(BF16)
    wd = w0_ref[...] + _dot(wl, w2_ref[...])
    al = _dot(mix(4), a1_ref[...]).astype(BF16)
    iclr2 = jax.nn.sigmoid(a0_ref[...] + _dot(al, a2_ref[...]))
    g = _dot(jax.nn.sigmoid(_dot(mix(5), g1_ref[...])).astype(BF16), g2_ref[...])
    ka = ka_ref[...]
    rk = rk_ref[...]
    a = -kk
    v_o[...] = v
    g_o[...] = g
    a_o[...] = a
    r_o[...] = r
    bonus = jnp.zeros_like(r)
    for dr in range(2):
        iclr = iclr2[:, dr * d:(dr + 1) * d]
        kd = k * (1.0 + (iclr - 1.0) * ka)
        lw_o[dr] = jax.nn.sigmoid(wd[:, dr * d:(dr + 1) * d]) * DECAY_SCALE
        k_o[dr] = kd
        b_o[dr] = kk * iclr
        bonus = bonus + r * kd * rk
    bonus_o[...] = _head_sum(bonus, s1, s2) * v


def _rwkv_proj(st, x, norm_g, mods, vfirst, p, grid_w, tm=256):
    has_vres = vfirst is not None
    tok = st.tok_spec(tm)
    hw = grid_w or CHUNK
    assert tm % hw == 0 and (grid_w or tm == st.l)
    rpt = tm // hw
    last = st.l // hw - 1
    up_spec = pl.BlockSpec((None, hw, st.d), lambda b, i: (b, jnp.maximum(i * rpt - 1, 0), 0))
    down_spec = pl.BlockSpec((None, hw, st.d), lambda b, i: (b, jnp.minimum((i + 1) * rpt, last), 0))
    ins = [x, x, x, norm_g.reshape(1, st.d), mods] + ([vfirst] if has_vres else [])
    specs = [tok, up_spec, down_spec, _const_spec((1, st.d)), st.mod_spec()] + ([tok] if has_vres else [])
    consts = [p['mu'], p['wr'], p['wk'], p['wv'], p['w1'], p['w2'], p['w0'],
              p['a1'], p['a2'], p['a0'], p['g1'], p['g2'], p['kk'], p['ka'], p['rk'],
              p['s1'], p['s2']]
    if has_vres:
        consts += [p['v0'], p['v1'], p['v2']]
    ins += consts
    specs += [_const_spec(c.shape) for c in consts]
    return pl.pallas_call(
        functools.partial(_rwkv_proj_kernel, has_vres, grid_w),
        grid=st.grid(tm),
        in_specs=specs,
        out_specs=[tok] * 5 + [st.dir_spec(tm)] * 3,
        out_shape=[st.shape()] * 5 + [st.dir_shape()] * 3,
        compiler_params=_cparams(_PAR2),
        name="rwkv_proj",
    )(*ins)


def _dot_nt(a, b):
    return lax.dot_general(a, b, (((1,), (1,)), ((), ())), preferred_element_type=F32)


def _split3(x):
    h1 = x.astype(BF16)
    r1 = x - h1.astype(F32)
    h2 = r1.astype(BF16)
    h3 = (r1 - h2.astype(F32)).astype(BF16)
    return h1, h2, h3


def _wkv_chunk_kernel(nb, a_ref, r_ref, v_ref, lw_ref, k_ref, b_ref, s0_ref, m4_ref,
                      y_ref, sf_ref, s_ref):
    dr = pl.program_id(0)
    ch = pl.program_id(1)
    ngroup = a_ref.shape[-1] // GROUP
    rev = dr == 1

    @pl.when(ch == 0)
    def _():
        s_ref[...] = s0_ref[...]

    m4 = m4_ref[...]
    row = lax.broadcasted_iota(jnp.int32, (CHUNK, GROUP), 0)
    col = lax.broadcasted_iota(jnp.int32, (CHUNK, GROUP), 1) % HEAD_DIM
    ahead = jnp.where(rev, row - col, col - row)
    strict = (ahead > 0).astype(F32)
    incl = (ahead >= 0).astype(F32)
    tr = lax.broadcasted_iota(jnp.int32, (CHUNK, CHUNK), 0)
    tc = lax.broadcasted_iota(jnp.int32, (CHUNK, CHUNK), 1)
    tri = (jnp.where(rev, tc - tr, tr - tc) >= 0).astype(F32).astype(BF16)

    def bd(x):
        xb = x.astype(BF16)
        return jnp.concatenate([xb, xb, xb, xb], axis=0) * m4

    def head_t(x):
        t = x.T
        return jnp.concatenate(
            [t[h * HEAD_DIM:(h + 1) * HEAD_DIM, :] for h in range(GROUP // HEAD_DIM)], axis=1)

    def body(it, carry):
        gs = range(ngroup * WKV_NB)
        bis = [it * WKV_NB + g // ngroup for g in gs]
        gi = [g % ngroup for g in gs]
        sl = [slice(gi[g] * GROUP, (gi[g] + 1) * GROUP) for g in gs]
        a = [a_ref[bis[g], :, sl[g]] for g in gs]
        r = [r_ref[bis[g], :, sl[g]] for g in gs]
        v = [v_ref[bis[g], :, sl[g]] for g in gs]
        lw = [lw_ref[bis[g], :, sl[g]] for g in gs]
        k = [k_ref[bis[g], :, sl[g]] for g in gs]
        b = [b_ref[bis[g], :, sl[g]] for g in gs]
        s0 = [s_ref[bis[g], gi[g]] for g in gs]
        cum = []
        for g in gs:
            l1, l2, l3 = _split3(lw[g])
            cum.append(_dot(tri, l1) + _dot(tri, l2) + _dot(tri, l3))
        at = [a[g] * jnp.exp(cum[g] - lw[g]) for g in gs]
        rt = [r[g] * jnp.exp(cum[g]) for g in gs]
        ginv = [jnp.exp(-cum[g]) for g in gs]
        bt = [b[g] * ginv[g] for g in gs]
        kt = [k[g] * ginv[g] for g in gs]
        gtot = [jnp.exp(jnp.where(rev, cum[g][0:1], cum[g][CHUNK - 1:CHUNK])) for g in gs]
        z = [jnp.concatenate([bt[g], kt[g], s0[g]], axis=0).astype(BF16) for g in gs]
        ma = [_dot_nt(z[g], bd(at[g])) for g in gs]
        mr = [_dot_nt(z[g], bd(rt[g])) for g in gs]
        vt = [head_t(v[g]) for g in gs]
        q = [ma[g][0:CHUNK] * strict for g in gs]
        u = [ma[g][2 * CHUNK:] + _dot(vt[g].astype(BF16), bd(ma[g][CHUNK:2 * CHUNK] * strict)) for g in gs]
        for j in range(6):
            for g in gs:
                if j < 5:
                    p = _dot(jnp.concatenate([u[g], q[g]], axis=0).astype(BF16), bd(q[g]))
                    u[g] = u[g] + p[0:CHUNK]
                    q[g] = p[CHUNK:]
                else:
                    u[g] = u[g] + _dot(u[g].astype(BF16), bd(q[g]))
        uv = [jnp.concatenate([u[g], vt[g]], axis=1).astype(BF16) for g in gs]
        yt = [mr[g][2 * CHUNK:] + _dot(uv[g], jnp.concatenate(
            [bd(mr[g][0:CHUNK] * incl), bd(mr[g][CHUNK:2 * CHUNK] * incl)], axis=0)) for g in gs]
        sn = [(s0[g] + _dot(uv[g], jnp.concatenate([bd(bt[g]), bd(kt[g])], axis=0))) * gtot[g]
              for g in gs]
        for g in gs:
            s_ref[bis[g], gi[g]] = sn[g]
            y_ref[bis[g], :, sl[g]] = head_t(yt[g])
        return carry

    lax.fori_loop(0, nb // WKV_NB, body, 0)

    @pl.when(ch == pl.num_programs(1) - 1)
    def _():
        sf_ref[...] = s_ref[...]


def _wkv(st, a, r, v, lw, k, b, s0):
    nb, l, d = st.b, st.l, st.d
    ngroup = d // GROUP
    nch = l // CHUNK
    assert l % CHUNK == 0 and d % GROUP == 0 and nb % WKV_NB == 0

    def chunk(dr, j):
        return jnp.where(dr == 1, nch - 1 - j, j)

    sh_spec = pl.BlockSpec((nb, CHUNK, d), lambda dr, j: (0, chunk(dr, j), 0))
    dir_spec = pl.BlockSpec((None, nb, CHUNK, d), lambda dr, j: (dr, 0, chunk(dr, j), 0))
    st_spec = pl.BlockSpec((None, nb, ngroup, HEAD_DIM, GROUP), lambda dr, j: (dr, 0, 0, 0, 0))
    blk = jnp.arange(GROUP) // HEAD_DIM
    m4 = (blk[:, None] == blk[None, :]).astype(BF16)
    return pl.pallas_call(
        functools.partial(_wkv_chunk_kernel, nb),
        grid=(2, nch),
        in_specs=[sh_spec, sh_spec, sh_spec, dir_spec, dir_spec, dir_spec, st_spec,
                  _const_spec(m4.shape)],
        out_specs=[dir_spec, st_spec],
        out_shape=[jax.ShapeDtypeStruct((2, nb, l, d), F32), jax.ShapeDtypeStruct(s0.shape, F32)],
        scratch_shapes=[pltpu.VMEM((nb, ngroup, HEAD_DIM, GROUP), F32)],
        compiler_params=_cparams(("arbitrary", "arbitrary")),
        name="wkv",
    )(a, r, v, lw, k, b, s0, m4)


def _rwkv_out_kernel(x_ref, y_ref, bonus_ref, g_ref, m_ref, lw_ref, lb_ref, wo_ref,
                     s1_ref, s2_ref, o_ref):
    s1 = s1_ref[...]
    s2 = s2_ref[...]
    y = y_ref[0] + y_ref[1]
    mean = _head_sum(y, s1, s2) * (1.0 / HEAD_DIM)
    yc = y - mean
    var = _head_sum(yc * yc, s1, s2) * (1.0 / HEAD_DIM)
    yn = (yc * lax.rsqrt(var + LNX_EPS)) * lw_ref[...] + lb_ref[...]
    out = (yn + bonus_ref[...]) * g_ref[...]
    o = _dot(out.astype(BF16), wo_ref[...])
    o_ref[...] = x_ref[...] + m_ref[...][2:3] * o


def _rwkv_out(st, x, y, bonus, g, mods, p, tm=256):
    tok = st.tok_spec(tm)
    consts = [p['lnx_w'], p['lnx_b'], p['wo'], p['s1'], p['s2']]
    return pl.pallas_call(
        _rwkv_out_kernel,
        grid=st.grid(tm),
        in_specs=[tok, st.dir_spec(tm), tok, tok, st.mod_spec()]
        + [_const_spec(c.shape) for c in consts],
        out_specs=tok,
        out_shape=st.shape(),
        compiler_params=_cparams(_PAR2),
        name="rwkv_out",
    )(x, y, bonus, g, mods, *consts)


def _mlp_kernel(final, x_ref, g_ref, m_ref, w1_ref, w2_ref, *rest):
    o_ref = rest[-1]
    x = x_ref[...]
    m = m_ref[...]
    hn = _norm_mod(x, g_ref[...], m[3:4], m[4:5]).astype(BF16)
    d = x.shape[-1]
    f = w1_ref.shape[-1]
    acc = jnp.zeros_like(x)
    for c in range(f // d):
        hid = jnp.maximum(_dot(hn, w1_ref[:, c * d:(c + 1) * d]), 0.0)
        acc = acc + _dot((hid * hid).astype(BF16), w2_ref[c * d:(c + 1) * d, :])
    y = x + m[5:6] * acc
    if final:
        ms = jnp.mean(y * y, axis=-1, keepdims=True)
        y = (y * lax.rsqrt(ms + RMS_EPS)) * rest[0][...]
    o_ref[...] = y


def _mlp(st, x, g, mods, w1, w2, tm=256, final_g=None):
    tm = min(tm, st.l)
    final = final_g is not None
    extra = [final_g.reshape(1, st.d)] if final else []
    return pl.pallas_call(
        functools.partial(_mlp_kernel, final),
        grid=st.grid(tm),
        in_specs=[st.tok_spec(tm), _const_spec((1, st.d)), st.mod_spec(),
                  _const_spec(w1.shape), _const_spec(w2.shape)] + [_const_spec((1, st.d))] * len(extra),
        out_specs=st.tok_spec(tm),
        out_shape=st.shape(),
        compiler_params=_cparams(_PAR2),
        name="mlp",
    )(x, g.reshape(1, st.d), mods, w1, w2, *extra)


def _dft_mats(n):
    idx = jnp.arange(n, dtype=jnp.int32)
    ang = ((idx[:, None] * idx[None, :]) % n).astype(F32) * (2.0 * jnp.pi / n)
    return jnp.cos(ang), jnp.sin(ang)


def _fnet_chan_kernel(x_ref, g_ref, m_ref, c_ref, s_ref, o_ref):
    m = m_ref[...]
    d = x_ref.shape[-1]
    cg = d // FOURIER_GROUPS
    h = _norm_mod(x_ref[...], g_ref[...], m[0:1], m[1:2]).astype(BF16)
    for gi in range(FOURIER_GROUPS):
        hg = h[:, gi * cg:(gi + 1) * cg]
        o_ref[:, gi * cg:(gi + 1) * cg] = _dot(hg, c_ref[...]).astype(BF16)
        o_ref[:, d + gi * cg:d + (gi + 1) * cg] = _dot(hg, s_ref[...]).astype(BF16)


def _fnet_chan(st, x, g, mods, cc, sc, tm=256):
    return pl.pallas_call(
        _fnet_chan_kernel,
        grid=st.grid(tm),
        in_specs=[st.tok_spec(tm), _const_spec((1, st.d)), st.mod_spec(),
                  _const_spec(cc.shape), _const_spec(sc.shape)],
        out_specs=st.tok_spec(tm, 2 * st.d),
        out_shape=st.shape(2 * st.d, BF16),
        compiler_params=_cparams(_PAR2),
        name="fnet_chan",
    )(x, g.reshape(1, st.d), mods, cc, sc)


def _fnet_seq_kernel(scale, x_ref, xcs_ref, cl_ref, sl_ref, m_ref, wo_ref, o_ref, acc_ref):
    kstep = pl.program_id(2)
    d = x_ref.shape[-1]

    @pl.when(kstep == 0)
    def _():
        acc_ref[...] = jnp.zeros_like(acc_ref)

    acc_ref[...] += _dot(cl_ref[...], xcs_ref[:, :d]) - _dot(sl_ref[...], xcs_ref[:, d:])

    @pl.when(kstep == pl.num_programs(2) - 1)
    def _():
        f = (acc_ref[...] * scale).astype(BF16)
        o_ref[...] = x_ref[...] + m_ref[...][2:3] * _dot(f, wo_ref[...])


def _fnet_seq(st, x, xcs, mods, wo, tl=1024, tk=512):
    d, l = st.d, st.l
    tl = min(tl, l)
    tk = min(tk, l)
    cl, sl = _dft_mats(l)
    cl = cl.astype(BF16)
    sl = sl.astype(BF16)
    scale = 1.0 / float(l * (d // FOURIER_GROUPS)) ** 0.5
    return pl.pallas_call(
        functools.partial(_fnet_seq_kernel, scale),
        grid=(st.b, l // tl, l // tk),
        in_specs=[
            pl.BlockSpec((None, tl, d), lambda b, i, k: (b, i, 0)),
            pl.BlockSpec((None, tk, 2 * d), lambda b, i, k: (b, k, 0)),
            pl.BlockSpec((tl, tk), lambda b, i, k: (i, k)),
            pl.BlockSpec((tl, tk), lambda b, i, k: (i, k)),
            st.mod_spec(),
            _const_spec(wo.shape),
        ],
        out_specs=pl.BlockSpec((None, tl, d), lambda b, i, k: (b, i, 0)),
        out_shape=st.shape(),
        scratch_shapes=[pltpu.VMEM((tl, d), F32)],
        compiler_params=_cparams(("parallel", "parallel", "arbitrary")),
        name="fnet_seq",
    )(x, xcs, cl, sl, mods, wo)


def _fft1_kernel(x_ref, g_ref, m_ref, cc_ref, sc_ref, w1_ref, tc_ref, ts_ref, o_ref):
    n, j_n, d = x_ref.shape
    cg = d // FOURIER_GROUPS
    m = m_ref[...]
    xs = jnp.concatenate([x_ref[:, j, :] for j in range(j_n)], axis=0)
    h = _norm_mod(xs, g_ref[...], m[0:1], m[1:2]).astype(BF16)
    xc = []
    xsn = []
    for gi in range(FOURIER_GROUPS):
        hg = h[:, gi * cg:(gi + 1) * cg]
        xc.append(_dot(hg, cc_ref[...]).astype(BF16))
        xsn.append(_dot(hg, sc_ref[...]).astype(BF16))
    xcs = jnp.concatenate(xc + xsn, axis=1)
    w1 = w1_ref[...]
    for j in range(j_n):
        p = _dot(w1, xcs[j * n:(j + 1) * n])
        a_re = p[:n, :d] - p[n:, d:]
        a_im = -(p[:n, d:] + p[n:, :d])
        tc = jnp.concatenate([tc_ref[j]] * (d // LANES), axis=1)
        ts = jnp.concatenate([ts_ref[j]] * (d // LANES), axis=1)
        o_ref[:, j, :d] = (a_re * tc + a_im * ts).astype(BF16)
        o_ref[:, j, d:] = (a_im * tc - a_re * ts).astype(BF16)


def _fft2_kernel(scale, x_ref, b_ref, m_ref, w2_ref, wo_ref, o_ref):
    n, j_n, d = x_ref.shape
    w2 = w2_ref[...]
    f = []
    for j in range(j_n):
        bj = b_ref[j]
        f.append(_dot(w2, jnp.concatenate([bj[:, :d], bj[:, d:]], axis=0)))
    fa = (jnp.concatenate(f, axis=0) * scale).astype(BF16)
    o = _dot(fa, wo_ref[...])
    ga = m_ref[...][2:3]
    for j in range(j_n):
        o_ref[:, j, :] = x_ref[:, j, :] + ga * o[j * n:(j + 1) * n]


def _fnet_fft(st, x, g, mods, cc, sc, wo):
    nb, l, d = st.b, st.l, st.d
    n, jb = FFT_N, FFT_J
    assert l == n * n
    c1, s1 = _dft_mats(n)
    w1 = jnp.concatenate([c1, s1], axis=0).astype(BF16)
    w2 = jnp.concatenate([c1, s1], axis=1).astype(BF16)
    idx = jnp.arange(n, dtype=jnp.int32)
    ang = (idx[:, None] * idx[None, :]).astype(F32) * (2.0 * jnp.pi / l)
    tc = jnp.broadcast_to(jnp.cos(ang)[:, :, None], (n, n, LANES))
    ts = jnp.broadcast_to(jnp.sin(ang)[:, :, None], (n, n, LANES))
    scale = 1.0 / float(l * (d // FOURIER_GROUPS)) ** 0.5
    x4 = x.reshape(nb, n, n, d)
    blk = pl.BlockSpec((None, n, jb, d), lambda b, i: (b, 0, i, 0))
    tw = pl.BlockSpec((jb, n, LANES), lambda b, i: (i, 0, 0))
    bmat = pl.pallas_call(
        _fft1_kernel,
        grid=(nb, n // jb),
        in_specs=[blk, _const_spec((1, d)), st.mod_spec(), _const_spec(cc.shape), _const_spec(sc.shape),
                  _const_spec(w1.shape), tw, tw],
        out_specs=pl.BlockSpec((None, n, jb, 2 * d), lambda b, i: (b, 0, i, 0)),
        out_shape=jax.ShapeDtypeStruct((nb, n, n, 2 * d), BF16),
        compiler_params=_cparams(_PAR2),
        name="fnet_fft1",
    )(x4, g.reshape(1, d), mods, cc, sc, w1, tc, ts)
    out = pl.pallas_call(
        functools.partial(_fft2_kernel, scale),
        grid=(nb, n // jb),
        in_specs=[blk, pl.BlockSpec((None, jb, n, 2 * d), lambda b, i: (b, i, 0, 0)),
                  st.mod_spec(), _const_spec(w2.shape), _const_spec(wo.shape)],
        out_specs=blk,
        out_shape=jax.ShapeDtypeStruct((nb, n, n, d), F32),
        compiler_params=_cparams(_PAR2),
        name="fnet_fft2",
    )(x4, bmat, mods, w2, wo)
    return out.reshape(nb, l, d)


def _blockdiag2(w):
    z = jnp.zeros_like(w[0])
    return jnp.concatenate([jnp.concatenate([w[0], z], axis=1),
                            jnp.concatenate([z, w[1]], axis=1)], axis=0)


def kernel(x, c, ctx, c_ctx, norm1_g, norm2_g, mod_w, mod_b, mlp_w1, mlp_w2, rk_mu, rk_wr, rk_wk, rk_wv, rk_wo, rk_w0, rk_w1, rk_w2, rk_a0, rk_a1, rk_a2, rk_v0, rk_v1, rk_v2, rk_g1, rk_g2, rk_kk, rk_ka, rk_rk, rk_lnx_w, rk_lnx_b, ft_wo, final_g):
    nb, seq, d = x.shape
    ctx_len = ctx.shape[1]
    depth = mod_w.shape[0]
    assert nb + 1 <= MOD_ROWS and d % LANES == 0
    lat = _Stream(nb, seq, d)
    cst = _Stream(nb, ctx_len, d, ctx_row=nb)
    row = lambda t: t.reshape(1, -1)

    cvec = jnp.zeros((MOD_ROWS, d), F32).at[:nb].set(c).at[nb].set(c_ctx)
    mods_all = _mod_table(cvec, mod_w, mod_b).reshape(depth, MOD_ROWS, N_MOD, d)

    heads = jnp.arange(d) // HEAD_DIM
    s1 = (heads[:, None] == jnp.arange(LANES)[None, :]).astype(BF16)
    s2 = s1.T
    cg = d // FOURIER_GROUPS
    cc, sc = _dft_mats(cg)
    cc = cc.astype(BF16)
    sc = sc.astype(BF16)
    zero_state = jnp.zeros((2, nb, d // GROUP, HEAD_DIM, GROUP), F32)

    last_a = ((depth - 1) // 2) * 2
    vf_lat = vf_ctx = None
    for i in range(depth):
        idx = i // 2
        mods = mods_all[i]
        ctx_in = i <= last_a
        ctx_out = i < last_a
        if i % 2 == 0:
            p = {
                'mu': rk_mu[idx], 'wr': rk_wr[idx].astype(BF16), 'wk': rk_wk[idx].astype(BF16),
                'wv': rk_wv[idx].astype(BF16), 'wo': rk_wo[idx].astype(BF16),
                'w1': jnp.concatenate([rk_w1[idx, 0], rk_w1[idx, 1]], axis=1).astype(BF16),
                'w2': _blockdiag2(rk_w2[idx]).astype(BF16), 'w0': row(rk_w0[idx]),
                'a1': jnp.concatenate([rk_a1[idx, 0], rk_a1[idx, 1]], axis=1).astype(BF16),
                'a2': _blockdiag2(rk_a2[idx]).astype(BF16), 'a0': row(rk_a0[idx]),
                'g1': rk_g1[idx].astype(BF16), 'g2': rk_g2[idx].astype(BF16),
                'kk': row(rk_kk[idx]), 'ka': row(rk_ka[idx]), 'rk': row(rk_rk[idx]),
                'lnx_w': row(rk_lnx_w[idx]), 'lnx_b': row(rk_lnx_b[idx]), 's1': s1, 's2': s2,
            }
            if idx > 0:
                p['v0'] = row(rk_v0[idx - 1])
                p['v1'] = rk_v1[idx - 1].astype(BF16)
                p['v2'] = rk_v2[idx - 1].astype(BF16)
            v_c, g_c, bonus_c, a_c, r_c, lw_c, k_c, b_c = _rwkv_proj(
                cst, ctx, norm1_g[i], mods, vf_ctx if idx > 0 else None, p, 0, tm=ctx_len)
            y_c, s_ctx = _wkv(cst, a_c, r_c, v_c, lw_c, k_c, b_c, zero_state)
            v_l, g_l, bonus_l, a_l, r_l, lw_l, k_l, b_l = _rwkv_proj(
                lat, x, norm1_g[i], mods, vf_lat if idx > 0 else None, p, GRID_W)
            y_l, _ = _wkv(lat, a_l, r_l, v_l, lw_l, k_l, b_l, s_ctx)
            if idx == 0:
                vf_ctx, vf_lat = v_c, v_l
            x = _rwkv_out(lat, x, y_l, bonus_l, g_l, mods, p)
            if ctx_out:
                ctx = _rwkv_out(cst, ctx, y_c, bonus_c, g_c, mods, p)
        else:
            wo = ft_wo[idx].astype(BF16)
            if seq == FFT_N * FFT_N:
                x = _fnet_fft(lat, x, norm1_g[i], mods, cc, sc, wo)
            else:
                x = _fnet_seq(lat, x, _fnet_chan(lat, x, norm1_g[i], mods, cc, sc), mods, wo)
            if ctx_out:
                ctx = _fnet_seq(cst, ctx, _fnet_chan(cst, ctx, norm1_g[i], mods, cc, sc), mods, wo)
        w1 = mlp_w1[i].astype(BF16)
        w2 = mlp_w2[i].astype(BF16)
        x = _mlp(lat, x, norm2_g[i], mods, w1, w2, tm=512,
                 final_g=final_g if i == depth - 1 else None)
        if ctx_out:
            ctx = _mlp(cst, ctx, norm2_g[i], mods, w1, w2)
    return x
```

```python
import functools

import jax
import jax.numpy as jnp
from jax import lax
from jax.experimental import pallas as pl
from jax.experimental.pallas import tpu as pltpu

HEAD_DIM = 64
GRID_W = 64
N_MOD = 6
FOURIER_GROUPS = 4
RMS_EPS = 1e-6
LNX_EPS = 64e-5
L2_EPS = 1e-24
DECAY_SCALE = -0.6065306597126334

MOD_ROWS = 16
LANES = 128
GROUP = 256
CHUNK = 64
WKV_NB = 4
FFT_N = 64
FFT_J = 8

VMEM_LIMIT = 56 * 1024 * 1024
TOKEN_TILE = 256
MLP_TILE = 512
DFT_OUT_TILE = 1024
DFT_K_TILE = 512

F32 = jnp.float32
BF16 = jnp.bfloat16


def _cparams(sem):
    return pltpu.CompilerParams(dimension_semantics=sem, vmem_limit_bytes=VMEM_LIMIT)


def _const_spec(shape):
    nd = len(shape)
    return pl.BlockSpec(shape, lambda *_: (0,) * nd, pipeline_mode=pl.Buffered(1))


def _dot(a, b):
    return jnp.dot(a, b, preferred_element_type=F32)


def _split2(x):
    hi = x.astype(BF16)
    lo = (x - hi.astype(F32)).astype(BF16)
    return hi, lo


def _head_sum(x, s1, s2):
    s = _dot(x.astype(BF16), s1)
    hi, lo = _split2(s)
    return _dot(hi, s2) + _dot(lo, s2)


def _norm_mod(x, g, sh, sc):
    ms = jnp.mean(x * x, axis=-1, keepdims=True)
    return (x * lax.rsqrt(ms + RMS_EPS)) * g * (1.0 + sc) + sh


def _mod_kernel(c_ref, w_ref, b_ref, o_ref):
    c = c_ref[...]
    s = c * jax.nn.sigmoid(c)
    o_ref[...] = jnp.dot(s, w_ref[...], precision=lax.Precision.HIGHEST,
                         preferred_element_type=F32) + b_ref[...]


def _mod_table(cvec, mod_w, mod_b):
    depth, d, nd = mod_w.shape
    nt = nd // d
    return pl.pallas_call(
        _mod_kernel,
        grid=(depth, nt),
        in_specs=[
            pl.BlockSpec((MOD_ROWS, d), lambda l, j: (0, 0)),
            pl.BlockSpec((None, d, d), lambda l, j: (l, 0, j)),
            pl.BlockSpec((None, 1, d), lambda l, j: (l, 0, j)),
        ],
        out_specs=pl.BlockSpec((None, MOD_ROWS, d), lambda l, j: (l, 0, j)),
        out_shape=jax.ShapeDtypeStruct((depth, MOD_ROWS, nd), F32),
        compiler_params=_cparams(("arbitrary", "arbitrary")),
        name="mod_table",
    )(cvec, mod_w, mod_b.reshape(depth, 1, nd))


class _Stream:
    def __init__(self, b, l, d, ctx_row=None):
        self.b, self.l, self.d, self.ctx_row = b, l, d, ctx_row

    def grid(self, tm):
        assert self.l % tm == 0
        return (self.b, self.l // tm)

    def tok_spec(self, tm, width=None):
        return pl.BlockSpec((None, tm, width or self.d), lambda b, i: (b, i, 0))

    def dir_spec(self, tm):
        return pl.BlockSpec((2, None, tm, self.d), lambda b, i: (0, b, i, 0))

    def dir_shape(self):
        return jax.ShapeDtypeStruct((2, self.b, self.l, self.d), F32)

    def mod_row(self, b):
        return b if self.ctx_row is None else b * 0 + self.ctx_row

    def mod_spec(self):
        return pl.BlockSpec((None, N_MOD, self.d), lambda b, *_: (self.mod_row(b), 0, 0))

    def shape(self, width=None, dtype=F32):
        return jax.ShapeDtypeStruct((self.b, self.l, width or self.d), dtype)


_PAR2 = ("parallel", "parallel")


def _grid_shift(h, h_up, h_down, grid_w):
    tm, d = h.shape
    q = d // 4
    col = lax.broadcasted_iota(jnp.int32, (tm, q), 0) % grid_w
    left = jnp.where(col == 0, 0.0, pltpu.roll(h[:, 0:q], 1, axis=0))
    right = jnp.where(col == grid_w - 1, 0.0, pltpu.roll(h[:, q:2 * q], tm - 1, axis=0))
    up = jnp.concatenate([h_up[:, 2 * q:3 * q], h[:tm - grid_w, 2 * q:3 * q]], axis=0)
    down = jnp.concatenate([h[grid_w:, 3 * q:], h_down[:, 3 * q:]], axis=0)
    return jnp.concatenate([left, right, up, down], axis=1)


def _line_shift(h):
    tm, d = h.shape
    half = d // 2
    t = lax.broadcasted_iota(jnp.int32, (tm, half), 0)
    prev = jnp.where(t == 0, 0.0, pltpu.roll(h[:, :half], 1, axis=0))
    nxt = jnp.where(t == tm - 1, 0.0, pltpu.roll(h[:, half:], tm - 1, axis=0))
    return jnp.concatenate([prev, nxt], axis=1)


def _rwkv_proj_kernel(has_vres, grid_w, *refs):
    if has_vres:
        (x_ref, xu_ref, xd_ref, ng_ref, m_ref, vf_ref, mu_ref, wr_ref, wk_ref, wv_ref, w1_ref, w2_ref, w0_ref,
         a1_ref, a2_ref, a0_ref, g1_ref, g2_ref, kk_ref, ka_ref, rk_ref, s1_ref, s2_ref,
         v0_ref, v1_ref, v2_ref,
         v_o, g_o, bonus_o, a_o, r_o, lw_o, k_o, b_o) = refs
    else:
        (x_ref, xu_ref, xd_ref, ng_ref, m_ref, mu_ref, wr_ref, wk_ref, wv_ref, w1_ref, w2_ref, w0_ref,
         a1_ref, a2_ref, a0_ref, g1_ref, g2_ref, kk_ref, ka_ref, rk_ref, s1_ref, s2_ref,
         v_o, g_o, bonus_o, a_o, r_o, lw_o, k_o, b_o) = refs
    d = x_ref.shape[-1]
    m = m_ref[...]
    ng = ng_ref[...]
    h = _norm_mod(x_ref[...], ng, m[0:1], m[1:2])
    if grid_w:
        i = pl.program_id(1)
        h_up = jnp.where(i > 0, _norm_mod(xu_ref[...], ng, m[0:1], m[1:2]), 0.0)
        h_down = jnp.where(i < pl.num_programs(1) - 1, _norm_mod(xd_ref[...], ng, m[0:1], m[1:2]), 0.0)
        xx = _grid_shift(h, h_up, h_down, grid_w) - h
    else:
        xx = _line_shift(h) - h
    mu = mu_ref[...]
    s1 = s1_ref[...]
    s2 = s2_ref[...]

    def mix(j):
        return (h + xx * mu[j:j + 1]).astype(BF16)

    r = _dot(mix(0), wr_ref[...])
    k = _dot(mix(2), wk_ref[...])
    xv = mix(3)
    v = _dot(xv, wv_ref[...])
    if has_vres:
        lo = _dot(_dot(xv, v1_ref[...]).astype(BF16), v2_ref[...])
        v = v + (vf_ref[...] - v) * jax.nn.sigmoid(v0_ref[...] + lo)
    kk = k * kk_ref[...]
    kk = kk * lax.rsqrt(jnp.maximum(_head_sum(kk * kk, s1, s2), L2_EPS))
    wl = jnp.tanh(_dot(mix(1), w1_ref[...])).astype(BF16)
    wd = w0_ref[...] + _dot(wl, w2_ref[...])
    al = _dot(mix(4), a1_ref[...]).astype(BF16)
    iclr2 = jax.nn.sigmoid(a0_ref[...] + _dot(al, a2_ref[...]))
    g = _dot(jax.nn.sigmoid(_dot(mix(5), g1_ref[...])).astype(BF16), g2_ref[...])
    ka = ka_ref[...]
    rk = rk_ref[...]
    a = -kk
    v_o[...] = v
    g_o[...] = g
    a_o[...] = a
    r_o[...] = r
    bonus = jnp.zeros_like(r)
    for dr in range(2):
        iclr = iclr2[:, dr * d:(dr + 1) * d]
        kd = k * (1.0 + (iclr - 1.0) * ka)
        lw_o[dr] = jax.nn.sigmoid(wd[:, dr * d:(dr + 1) * d]) * DECAY_SCALE
        k_o[dr] = kd
        b_o[dr] = kk * iclr
        bonus = bonus + r * kd * rk
    bonus_o[...] = _head_sum(bonus, s1, s2) * v


def _rwkv_proj(st, x, norm_g, mods, vfirst, p, grid_w, tm=TOKEN_TILE):
    has_vres = vfirst is not None
    tok = st.tok_spec(tm)
    hw = grid_w or CHUNK
    assert tm % hw == 0 and (grid_w or tm == st.l)
    rpt = tm // hw
    last = st.l // hw - 1
    up_spec = pl.BlockSpec((None, hw, st.d), lambda b, i: (b, jnp.maximum(i * rpt - 1, 0), 0))
    down_spec = pl.BlockSpec((None, hw, st.d), lambda b, i: (b, jnp.minimum((i + 1) * rpt, last), 0))
    ins = [x, x, x, norm_g.reshape(1, st.d), mods] + ([vfirst] if has_vres else [])
    specs = [tok, up_spec, down_spec, _const_spec((1, st.d)), st.mod_spec()] + ([tok] if has_vres else [])
    consts = [p['mu'], p['wr'], p['wk'], p['wv'], p['w1'], p['w2'], p['w0'],
              p['a1'], p['a2'], p['a0'], p['g1'], p['g2'], p['kk'], p['ka'], p['rk'],
              p['s1'], p['s2']]
    if has_vres:
        consts += [p['v0'], p['v1'], p['v2']]
    ins += consts
    specs += [_const_spec(c.shape) for c in consts]
    return pl.pallas_call(
        functools.partial(_rwkv_proj_kernel, has_vres, grid_w),
        grid=st.grid(tm),
        in_specs=specs,
        out_specs=[tok] * 5 + [st.dir_spec(tm)] * 3,
        out_shape=[st.shape()] * 5 + [st.dir_shape()] * 3,
        compiler_params=_cparams(_PAR2),
        name="rwkv_proj",
    )(*ins)


def _dot_nt(a, b):
    return lax.dot_general(a, b, (((1,), (1,)), ((), ())), preferred_element_type=F32)


def _wkv_chunk_kernel(nb, a_ref, r_ref, v_ref, lw_ref, k_ref, b_ref, s0_ref, m4_ref,
                      y_ref, sf_ref, s_ref):
    dr = pl.program_id(0)
    ch = pl.program_id(1)
    ngroup = a_ref.shape[-1] // GROUP
    rev = dr == 1

    @pl.when(ch == 0)
    def _():
        s_ref[...] = s0_ref[...]

    m4 = m4_ref[...]
    row = lax.broadcasted_iota(jnp.int32, (CHUNK, GROUP), 0)
    col = lax.broadcasted_iota(jnp.int32, (CHUNK, GROUP), 1) % HEAD_DIM
    ahead = jnp.where(rev, row - col, col - row)
    strict = (ahead > 0).astype(F32)
    incl = (ahead >= 0).astype(F32)

    def bd(x):
        xb = x.astype(BF16)
        return jnp.concatenate([xb, xb, xb, xb], axis=0) * m4

    def head_t(x):
        t = x.T
        return jnp.concatenate(
            [t[h * HEAD_DIM:(h + 1) * HEAD_DIM, :] for h in range(GROUP // HEAD_DIM)], axis=1)

    def run_sum(x):
        p = x
        shift = 1
        while shift < CHUNK:
            p = p + jnp.where(row >= shift, pltpu.roll(p, shift, axis=0), 0.0)
            shift *= 2
        return jnp.where(rev, p[CHUNK - 1:CHUNK] - p + x, p)

    def body(it, carry):
        gs = range(ngroup * WKV_NB)
        bis = [it * WKV_NB + g // ngroup for g in gs]
        gi = [g % ngroup for g in gs]
        sl = [slice(gi[g] * GROUP, (gi[g] + 1) * GROUP) for g in gs]
        a = [a_ref[bis[g], :, sl[g]] for g in gs]
        r = [r_ref[bis[g], :, sl[g]] for g in gs]
        v = [v_ref[bis[g], :, sl[g]] for g in gs]
        lw = [lw_ref[bis[g], :, sl[g]] for g in gs]
        k = [k_ref[bis[g], :, sl[g]] for g in gs]
        b = [b_ref[bis[g], :, sl[g]] for g in gs]
        s0 = [s_ref[bis[g], gi[g]] for g in gs]
        cum = [run_sum(lw[g]) for g in gs]
        at = [a[g] * jnp.exp(cum[g] - lw[g]) for g in gs]
        rt = [r[g] * jnp.exp(cum[g]) for g in gs]
        ginv = [jnp.exp(-cum[g]) for g in gs]
        bt = [b[g] * ginv[g] for g in gs]
        kt = [k[g] * ginv[g] for g in gs]
        gtot = [jnp.exp(jnp.where(rev, cum[g][0:1], cum[g][CHUNK - 1:CHUNK])) for g in gs]
        z = [jnp.concatenate([bt[g], kt[g], s0[g]], axis=0).astype(BF16) for g in gs]
        ma = [_dot_nt(z[g], bd(at[g])) for g in gs]
        mr = [_dot_nt(z[g], bd(rt[g])) for g in gs]
        vt = [head_t(v[g]) for g in gs]
        q = [ma[g][0:CHUNK] * strict for g in gs]
        u = [ma[g][2 * CHUNK:] + _dot(vt[g].astype(BF16), bd(ma[g][CHUNK:2 * CHUNK] * strict)) for g in gs]
        for j in range(6):
            for g in gs:
                if j < 5:
                    p = _dot(jnp.concatenate([u[g], q[g]], axis=0).astype(BF16), bd(q[g]))
                    u[g] = u[g] + p[0:CHUNK]
                    q[g] = p[CHUNK:]
                else:
                    u[g] = u[g] + _dot(u[g].astype(BF16), bd(q[g]))
        uv = [jnp.concatenate([u[g], vt[g]], axis=1).astype(BF16) for g in gs]
        yt = [mr[g][2 * CHUNK:] + _dot(uv[g], jnp.concatenate(
            [bd(mr[g][0:CHUNK] * incl), bd(mr[g][CHUNK:2 * CHUNK] * incl)], axis=0)) for g in gs]
        sn = [(s0[g] + _dot(uv[g], jnp.concatenate([bd(bt[g]), bd(kt[g])], axis=0))) * gtot[g]
              for g in gs]
        for g in gs:
            s_ref[bis[g], gi[g]] = sn[g]
            y_ref[bis[g], :, sl[g]] = head_t(yt[g])
        return carry

    lax.fori_loop(0, nb // WKV_NB, body, 0)

    @pl.when(ch == pl.num_programs(1) - 1)
    def _():
        sf_ref[...] = s_ref[...]


def _wkv(st, a, r, v, lw, k, b, s0):
    nb, l, d = st.b, st.l, st.d
    ngroup = d // GROUP
    nch = l // CHUNK
    assert l % CHUNK == 0 and d % GROUP == 0 and nb % WKV_NB == 0

    def chunk(dr, j):
        return jnp.where(dr == 1, nch - 1 - j, j)

    sh_spec = pl.BlockSpec((nb, CHUNK, d), lambda dr, j: (0, chunk(dr, j), 0))
    dir_spec = pl.BlockSpec((None, nb, CHUNK, d), lambda dr, j: (dr, 0, chunk(dr, j), 0))
    st_spec = pl.BlockSpec((None, nb, ngroup, HEAD_DIM, GROUP), lambda dr, j: (dr, 0, 0, 0, 0))
    blk = jnp.arange(GROUP) // HEAD_DIM
    m4 = (blk[:, None] == blk[None, :]).astype(BF16)
    return pl.pallas_call(
        functools.partial(_wkv_chunk_kernel, nb),
        grid=(2, nch),
        in_specs=[sh_spec, sh_spec, sh_spec, dir_spec, dir_spec, dir_spec, st_spec,
                  _const_spec(m4.shape)],
        out_specs=[dir_spec, st_spec],
        out_shape=[jax.ShapeDtypeStruct((2, nb, l, d), F32), jax.ShapeDtypeStruct(s0.shape, F32)],
        scratch_shapes=[pltpu.VMEM((nb, ngroup, HEAD_DIM, GROUP), F32)],
        compiler_params=_cparams(("arbitrary", "arbitrary")),
        name="wkv",
    )(a, r, v, lw, k, b, s0, m4)


def _rwkv_out_kernel(x_ref, y_ref, bonus_ref, g_ref, m_ref, lw_ref, lb_ref, wo_ref,
                     s1_ref, s2_ref, o_ref):
    s1 = s1_ref[...]
    s2 = s2_ref[...]
    y = y_ref[0] + y_ref[1]
    mean = _head_sum(y, s1, s2) * (1.0 / HEAD_DIM)
    yc = y - mean
    var = _head_sum(yc * yc, s1, s2) * (1.0 / HEAD_DIM)
    yn = (yc * lax.rsqrt(var + LNX_EPS)) * lw_ref[...] + lb_ref[...]
    out = (yn + bonus_ref[...]) * g_ref[...]
    o = _dot(out.astype(BF16), wo_ref[...])
    o_ref[...] = x_ref[...] + m_ref[...][2:3] * o


def _rwkv_out(st, x, y, bonus, g, mods, p, tm=TOKEN_TILE):
    tok = st.tok_spec(tm)
    consts = [p['lnx_w'], p['lnx_b'], p['wo'], p['s1'], p['s2']]
    return pl.pallas_call(
        _rwkv_out_kernel,
        grid=st.grid(tm),
        in_specs=[tok, st.dir_spec(tm), tok, tok, st.mod_spec()]
        + [_const_spec(c.shape) for c in consts],
        out_specs=tok,
        out_shape=st.shape(),
        compiler_params=_cparams(_PAR2),
        name="rwkv_out",
    )(x, y, bonus, g, mods, *consts)


def _mlp_kernel(final, x_ref, g_ref, m_ref, w1_ref, w2_ref, *rest):
    o_ref = rest[-1]
    x = x_ref[...]
    m = m_ref[...]
    hn = _norm_mod(x, g_ref[...], m[3:4], m[4:5]).astype(BF16)
    d = x.shape[-1]
    f = w1_ref.shape[-1]
    acc = jnp.zeros_like(x)
    for c in range(f // d):
        hid = jnp.maximum(_dot(hn, w1_ref[:, c * d:(c + 1) * d]), 0.0)
        acc = acc + _dot((hid * hid).astype(BF16), w2_ref[c * d:(c + 1) * d, :])
    y = x + m[5:6] * acc
    if final:
        ms = jnp.mean(y * y, axis=-1, keepdims=True)
        y = (y * lax.rsqrt(ms + RMS_EPS)) * rest[0][...]
    o_ref[...] = y


def _mlp(st, x, g, mods, w1, w2, tm=TOKEN_TILE, final_g=None):
    tm = min(tm, st.l)
    final = final_g is not None
    extra = [final_g.reshape(1, st.d)] if final else []
    return pl.pallas_call(
        functools.partial(_mlp_kernel, final),
        grid=st.grid(tm),
        in_specs=[st.tok_spec(tm), _const_spec((1, st.d)), st.mod_spec(),
                  _const_spec(w1.shape), _const_spec(w2.shape)] + [_const_spec((1, st.d))] * len(extra),
        out_specs=st.tok_spec(tm),
        out_shape=st.shape(),
        compiler_params=_cparams(_PAR2),
        name="mlp",
    )(x, g.reshape(1, st.d), mods, w1, w2, *extra)


def _dft_mats(n):
    idx = jnp.arange(n, dtype=jnp.int32)
    ang = ((idx[:, None] * idx[None, :]) % n).astype(F32) * (2.0 * jnp.pi / n)
    return jnp.cos(ang), jnp.sin(ang)


def _fnet_chan_kernel(x_ref, g_ref, m_ref, c_ref, s_ref, o_ref):
    m = m_ref[...]
    d = x_ref.shape[-1]
    cg = d // FOURIER_GROUPS
    h = _norm_mod(x_ref[...], g_ref[...], m[0:1], m[1:2]).astype(BF16)
    for gi in range(FOURIER_GROUPS):
        hg = h[:, gi * cg:(gi + 1) * cg]
        o_ref[:, gi * cg:(gi + 1) * cg] = _dot(hg, c_ref[...]).astype(BF16)
        o_ref[:, d + gi * cg:d + (gi + 1) * cg] = _dot(hg, s_ref[...]).astype(BF16)


def _fnet_chan(st, x, g, mods, cc, sc, tm=TOKEN_TILE):
    return pl.pallas_call(
        _fnet_chan_kernel,
        grid=st.grid(tm),
        in_specs=[st.tok_spec(tm), _const_spec((1, st.d)), st.mod_spec(),
                  _const_spec(cc.shape), _const_spec(sc.shape)],
        out_specs=st.tok_spec(tm, 2 * st.d),
        out_shape=st.shape(2 * st.d, BF16),
        compiler_params=_cparams(_PAR2),
        name="fnet_chan",
    )(x, g.reshape(1, st.d), mods, cc, sc)


def _fnet_seq_kernel(scale, x_ref, xcs_ref, cl_ref, sl_ref, m_ref, wo_ref, o_ref, acc_ref):
    kstep = pl.program_id(2)
    d = x_ref.shape[-1]

    @pl.when(kstep == 0)
    def _():
        acc_ref[...] = jnp.zeros_like(acc_ref)

    acc_ref[...] += _dot(cl_ref[...], xcs_ref[:, :d]) - _dot(sl_ref[...], xcs_ref[:, d:])

    @pl.when(kstep == pl.num_programs(2) - 1)
    def _():
        f = (acc_ref[...] * scale).astype(BF16)
        o_ref[...] = x_ref[...] + m_ref[...][2:3] * _dot(f, wo_ref[...])


def _fnet_seq(st, x, xcs, mods, wo, tl=DFT_OUT_TILE, tk=DFT_K_TILE):
    d, l = st.d, st.l
    tl = min(tl, l)
    tk = min(tk, l)
    cl, sl = _dft_mats(l)
    cl = cl.astype(BF16)
    sl = sl.astype(BF16)
    scale = 1.0 / float(l * (d // FOURIER_GROUPS)) ** 0.5
    return pl.pallas_call(
        functools.partial(_fnet_seq_kernel, scale),
        grid=(st.b, l // tl, l // tk),
        in_specs=[
            pl.BlockSpec((None, tl, d), lambda b, i, k: (b, i, 0)),
            pl.BlockSpec((None, tk, 2 * d), lambda b, i, k: (b, k, 0)),
            pl.BlockSpec((tl, tk), lambda b, i, k: (i, k)),
            pl.BlockSpec((tl, tk), lambda b, i, k: (i, k)),
            st.mod_spec(),
            _const_spec(wo.shape),
        ],
        out_specs=pl.BlockSpec((None, tl, d), lambda b, i, k: (b, i, 0)),
        out_shape=st.shape(),
        scratch_shapes=[pltpu.VMEM((tl, d), F32)],
        compiler_params=_cparams(("parallel", "parallel", "arbitrary")),
        name="fnet_seq",
    )(x, xcs, cl, sl, mods, wo)


def _fft1_kernel(x_ref, g_ref, m_ref, cc_ref, sc_ref, w1_ref, tc_ref, ts_ref, o_ref):
    n, j_n, d = x_ref.shape
    cg = d // FOURIER_GROUPS
    m = m_ref[...]
    xs = jnp.concatenate([x_ref[:, j, :] for j in range(j_n)], axis=0)
    h = _norm_mod(xs, g_ref[...], m[0:1], m[1:2]).astype(BF16)
    xc = []
    xsn = []
    for gi in range(FOURIER_GROUPS):
        hg = h[:, gi * cg:(gi + 1) * cg]
        xc.append(_dot(hg, cc_ref[...]).astype(BF16))
        xsn.append(_dot(hg, sc_ref[...]).astype(BF16))
    xcs = jnp.concatenate(xc + xsn, axis=1)
    w1 = w1_ref[...]
    for j in range(j_n):
        p = _dot(w1, xcs[j * n:(j + 1) * n])
        a_re = p[:n, :d] - p[n:, d:]
        a_im = -(p[:n, d:] + p[n:, :d])
        tc = jnp.concatenate([tc_ref[j]] * (d // LANES), axis=1)
        ts = jnp.concatenate([ts_ref[j]] * (d // LANES), axis=1)
        o_ref[:, j, :d] = (a_re * tc + a_im * ts).astype(BF16)
        o_ref[:, j, d:] = (a_im * tc - a_re * ts).astype(BF16)


def _fft2_kernel(scale, x_ref, b_ref, m_ref, w2_ref, wo_ref, o_ref):
    n, j_n, d = x_ref.shape
    w2 = w2_ref[...]
    f = []
    for j in range(j_n):
        bj = b_ref[j]
        f.append(_dot(w2, jnp.concatenate([bj[:, :d], bj[:, d:]], axis=0)))
    fa = (jnp.concatenate(f, axis=0) * scale).astype(BF16)
    o = _dot(fa, wo_ref[...])
    ga = m_ref[...][2:3]
    for j in range(j_n):
        o_ref[:, j, :] = x_ref[:, j, :] + ga * o[j * n:(j + 1) * n]


def _fnet_fft(st, x, g, mods, cc, sc, wo):
    nb, l, d = st.b, st.l, st.d
    n, jb = FFT_N, FFT_J
    assert l == n * n
    c1, s1 = _dft_mats(n)
    w1 = jnp.concatenate([c1, s1], axis=0).astype(BF16)
    w2 = jnp.concatenate([c1, s1], axis=1).astype(BF16)
    idx = jnp.arange(n, dtype=jnp.int32)
    ang = (idx[:, None] * idx[None, :]).astype(F32) * (2.0 * jnp.pi / l)
    tc = jnp.broadcast_to(jnp.cos(ang)[:, :, None], (n, n, LANES))
    ts = jnp.broadcast_to(jnp.sin(ang)[:, :, None], (n, n, LANES))
    scale = 1.0 / float(l * (d // FOURIER_GROUPS)) ** 0.5
    x4 = x.reshape(nb, n, n, d)
    blk = pl.BlockSpec((None, n, jb, d), lambda b, i: (b, 0, i, 0))
    tw = pl.BlockSpec((jb, n, LANES), lambda b, i: (i, 0, 0))
    bmat = pl.pallas_call(
        _fft1_kernel,
        grid=(nb, n // jb),
        in_specs=[blk, _const_spec((1, d)), st.mod_spec(), _const_spec(cc.shape), _const_spec(sc.shape),
                  _const_spec(w1.shape), tw, tw],
        out_specs=pl.BlockSpec((None, n, jb, 2 * d), lambda b, i: (b, 0, i, 0)),
        out_shape=jax.ShapeDtypeStruct((nb, n, n, 2 * d), BF16),
        compiler_params=_cparams(_PAR2),
        name="fnet_fft1",
    )(x4, g.reshape(1, d), mods, cc, sc, w1, tc, ts)
    out = pl.pallas_call(
        functools.partial(_fft2_kernel, scale),
        grid=(nb, n // jb),
        in_specs=[blk, pl.BlockSpec((None, jb, n, 2 * d), lambda b, i: (b, i, 0, 0)),
                  st.mod_spec(), _const_spec(w2.shape), _const_spec(wo.shape)],
        out_specs=blk,
        out_shape=jax.ShapeDtypeStruct((nb, n, n, d), F32),
        compiler_params=_cparams(_PAR2),
        name="fnet_fft2",
    )(x4, bmat, mods, w2, wo)
    return out.reshape(nb, l, d)


def _blockdiag2(w):
    z = jnp.zeros_like(w[0])
    return jnp.concatenate([jnp.concatenate([w[0], z], axis=1),
                            jnp.concatenate([z, w[1]], axis=1)], axis=0)


def kernel(x, c, ctx, c_ctx, norm1_g, norm2_g, mod_w, mod_b, mlp_w1, mlp_w2, rk_mu, rk_wr, rk_wk, rk_wv, rk_wo, rk_w0, rk_w1, rk_w2, rk_a0, rk_a1, rk_a2, rk_v0, rk_v1, rk_v2, rk_g1, rk_g2, rk_kk, rk_ka, rk_rk, rk_lnx_w, rk_lnx_b, ft_wo, final_g):
    nb, seq, d = x.shape
    ctx_len = ctx.shape[1]
    depth = mod_w.shape[0]
    assert nb + 1 <= MOD_ROWS and d % LANES == 0
    lat = _Stream(nb, seq, d)
    cst = _Stream(nb, ctx_len, d, ctx_row=nb)
    row = lambda t: t.reshape(1, -1)

    cvec = jnp.zeros((MOD_ROWS, d), F32).at[:nb].set(c).at[nb].set(c_ctx)
    mods_all = _mod_table(cvec, mod_w, mod_b).reshape(depth, MOD_ROWS, N_MOD, d)

    heads = jnp.arange(d) // HEAD_DIM
    s1 = (heads[:, None] == jnp.arange(LANES)[None, :]).astype(BF16)
    s2 = s1.T
    cg = d // FOURIER_GROUPS
    cc, sc = _dft_mats(cg)
    cc = cc.astype(BF16)
    sc = sc.astype(BF16)
    zero_state = jnp.zeros((2, nb, d // GROUP, HEAD_DIM, GROUP), F32)

    last_a = ((depth - 1) // 2) * 2
    vf_lat = vf_ctx = None
    for i in range(depth):
        idx = i // 2
        mods = mods_all[i]
        ctx_out = i < last_a
        if i % 2 == 0:
            p = {
                'mu': rk_mu[idx], 'wr': rk_wr[idx].astype(BF16), 'wk': rk_wk[idx].astype(BF16),
                'wv': rk_wv[idx].astype(BF16), 'wo': rk_wo[idx].astype(BF16),
                'w1': jnp.concatenate([rk_w1[idx, 0], rk_w1[idx, 1]], axis=1).astype(BF16),
                'w2': _blockdiag2(rk_w2[idx]).astype(BF16), 'w0': row(rk_w0[idx]),
                'a1': jnp.concatenate([rk_a1[idx, 0], rk_a1[idx, 1]], axis=1).astype(BF16),
                'a2': _blockdiag2(rk_a2[idx]).astype(BF16), 'a0': row(rk_a0[idx]),
                'g1': rk_g1[idx].astype(BF16), 'g2': rk_g2[idx].astype(BF16),
                'kk': row(rk_kk[idx]), 'ka': row(rk_ka[idx]), 'rk': row(rk_rk[idx]),
                'lnx_w': row(rk_lnx_w[idx]), 'lnx_b': row(rk_lnx_b[idx]), 's1': s1, 's2': s2,
            }
            if idx > 0:
                p['v0'] = row(rk_v0[idx - 1])
                p['v1'] = rk_v1[idx - 1].astype(BF16)
                p['v2'] = rk_v2[idx - 1].astype(BF16)
            v_c, g_c, bonus_c, a_c, r_c, lw_c, k_c, b_c = _rwkv_proj(
                cst, ctx, norm1_g[i], mods, vf_ctx if idx > 0 else None, p, 0, tm=ctx_len)
            y_c, s_ctx = _wkv(cst, a_c, r_c, v_c, lw_c, k_c, b_c, zero_state)
            v_l, g_l, bonus_l, a_l, r_l, lw_l, k_l, b_l = _rwkv_proj(
                lat, x, norm1_g[i], mods, vf_lat if idx > 0 else None, p, GRID_W)
            y_l, _ = _wkv(lat, a_l, r_l, v_l, lw_l, k_l, b_l, s_ctx)
            if idx == 0:
                vf_ctx, vf_lat = v_c, v_l
            x = _rwkv_out(lat, x, y_l, bonus_l, g_l, mods, p)
            if ctx_out:
                ctx = _rwkv_out(cst, ctx, y_c, bonus_c, g_c, mods, p)
        else:
            wo = ft_wo[idx].astype(BF16)
            if seq == FFT_N * FFT_N:
                x = _fnet_fft(lat, x, norm1_g[i], mods, cc, sc, wo)
            else:
                x = _fnet_seq(lat, x, _fnet_chan(lat, x, norm1_g[i], mods, cc, sc), mods, wo)
            if ctx_out:
                ctx = _fnet_seq(cst, ctx, _fnet_chan(cst, ctx, norm1_g[i], mods, cc, sc), mods, wo)
        w1 = mlp_w1[i].astype(BF16)
        w2 = mlp_w2[i].astype(BF16)
        x = _mlp(lat, x, norm2_g[i], mods, w1, w2, tm=MLP_TILE,
                 final_g=final_g if i == depth - 1 else None)
        if ctx_out:
            ctx = _mlp(cst, ctx, norm2_g[i], mods, w1, w2)
    return x
```

```python
import functools

import jax
import jax.numpy as jnp
from jax import lax
from jax.experimental import pallas as pl
from jax.experimental.pallas import tpu as pltpu

HEAD_DIM = 64
GRID_W = 64
N_MOD = 6
FOURIER_GROUPS = 4
RMS_EPS = 1e-6
LNX_EPS = 64e-5
L2_EPS = 1e-24
DECAY_SCALE = -0.6065306597126334

MOD_ROWS = 16
LANES = 128
GROUP = 256
CHUNK = 64
WKV_NB = 4
FFT_N = 64
FFT_J = 8

VMEM_LIMIT = 56 * 1024 * 1024
TOKEN_TILE = 256
DFT_OUT_TILE = 1024
DFT_K_TILE = 512

F32 = jnp.float32
BF16 = jnp.bfloat16


def _cparams(sem):
    return pltpu.CompilerParams(dimension_semantics=sem, vmem_limit_bytes=VMEM_LIMIT)


def _const_spec(shape):
    nd = len(shape)
    return pl.BlockSpec(shape, lambda *_: (0,) * nd, pipeline_mode=pl.Buffered(1))


def _dot(a, b):
    return jnp.dot(a, b, preferred_element_type=F32)


def _split2(x):
    hi = x.astype(BF16)
    lo = (x - hi.astype(F32)).astype(BF16)
    return hi, lo


def _head_sum(x, s1, s2):
    s = _dot(x.astype(BF16), s1)
    hi, lo = _split2(s)
    return _dot(hi, s2) + _dot(lo, s2)


def _norm_mod(x, g, sh, sc):
    ms = jnp.mean(x * x, axis=-1, keepdims=True)
    return (x * lax.rsqrt(ms + RMS_EPS)) * g * (1.0 + sc) + sh


def _mod_kernel(c_ref, w_ref, b_ref, o_ref):
    c = c_ref[...]
    s = c * jax.nn.sigmoid(c)
    o_ref[...] = jnp.dot(s, w_ref[...], precision=lax.Precision.HIGHEST,
                         preferred_element_type=F32) + b_ref[...]


def _mod_table(cvec, mod_w, mod_b):
    depth, d, nd = mod_w.shape
    nt = nd // d
    return pl.pallas_call(
        _mod_kernel,
        grid=(depth, nt),
        in_specs=[
            pl.BlockSpec((MOD_ROWS, d), lambda l, j: (0, 0)),
            pl.BlockSpec((None, d, d), lambda l, j: (l, 0, j)),
            pl.BlockSpec((None, 1, d), lambda l, j: (l, 0, j)),
        ],
        out_specs=pl.BlockSpec((None, MOD_ROWS, d), lambda l, j: (l, 0, j)),
        out_shape=jax.ShapeDtypeStruct((depth, MOD_ROWS, nd), F32),
        compiler_params=_cparams(("arbitrary", "arbitrary")),
        name="mod_table",
    )(cvec, mod_w, mod_b.reshape(depth, 1, nd))


class _Stream:
    def __init__(self, b, l, d, ctx_row=None):
        self.b, self.l, self.d, self.ctx_row = b, l, d, ctx_row

    def grid(self, tm):
        assert self.l % tm == 0
        return (self.b, self.l // tm)

    def tok_spec(self, tm, width=None):
        return pl.BlockSpec((None, tm, width or self.d), lambda b, i: (b, i, 0))

    def dir_spec(self, tm):
        return pl.BlockSpec((2, None, tm, self.d), lambda b, i: (0, b, i, 0))

    def dir_shape(self):
        return jax.ShapeDtypeStruct((2, self.b, self.l, self.d), F32)

    def mod_row(self, b):
        return b if self.ctx_row is None else b * 0 + self.ctx_row

    def mod_spec(self):
        return pl.BlockSpec((None, N_MOD, self.d), lambda b, *_: (self.mod_row(b), 0, 0))

    def shape(self, width=None, dtype=F32):
        return jax.ShapeDtypeStruct((self.b, self.l, width or self.d), dtype)


_PAR2 = ("parallel", "parallel")


def _grid_shift(h, h_up, h_down, grid_w):
    tm, d = h.shape
    q = d // 4
    col = lax.broadcasted_iota(jnp.int32, (tm, q), 0) % grid_w
    left = jnp.where(col == 0, 0.0, pltpu.roll(h[:, 0:q], 1, axis=0))
    right = jnp.where(col == grid_w - 1, 0.0, pltpu.roll(h[:, q:2 * q], tm - 1, axis=0))
    up = jnp.concatenate([h_up[:, 2 * q:3 * q], h[:tm - grid_w, 2 * q:3 * q]], axis=0)
    down = jnp.concatenate([h[grid_w:, 3 * q:], h_down[:, 3 * q:]], axis=0)
    return jnp.concatenate([left, right, up, down], axis=1)


def _line_shift(h):
    tm, d = h.shape
    half = d // 2
    t = lax.broadcasted_iota(jnp.int32, (tm, half), 0)
    prev = jnp.where(t == 0, 0.0, pltpu.roll(h[:, :half], 1, axis=0))
    nxt = jnp.where(t == tm - 1, 0.0, pltpu.roll(h[:, half:], tm - 1, axis=0))
    return jnp.concatenate([prev, nxt], axis=1)


def _rwkv_proj_kernel(has_vres, grid_w, *refs):
    if has_vres:
        (x_ref, xu_ref, xd_ref, ng_ref, m_ref, vf_ref, mu_ref, wr_ref, wk_ref, wv_ref, w1_ref, w2_ref, w0_ref,
         a1_ref, a2_ref, a0_ref, g1_ref, g2_ref, kk_ref, ka_ref, rk_ref, s1_ref, s2_ref,
         v0_ref, v1_ref, v2_ref,
         v_o, g_o, bonus_o, a_o, r_o, lw_o, k_o, b_o) = refs
    else:
        (x_ref, xu_ref, xd_ref, ng_ref, m_ref, mu_ref, wr_ref, wk_ref, wv_ref, w1_ref, w2_ref, w0_ref,
         a1_ref, a2_ref, a0_ref, g1_ref, g2_ref, kk_ref, ka_ref, rk_ref, s1_ref, s2_ref,
         v_o, g_o, bonus_o, a_o, r_o, lw_o, k_o, b_o) = refs
    d = x_ref.shape[-1]
    m = m_ref[...]
    ng = ng_ref[...]
    h = _norm_mod(x_ref[...], ng, m[0:1], m[1:2])
    if grid_w:
        i = pl.program_id(1)
        h_up = jnp.where(i > 0, _norm_mod(xu_ref[...], ng, m[0:1], m[1:2]), 0.0)
        h_down = jnp.where(i < pl.num_programs(1) - 1, _norm_mod(xd_ref[...], ng, m[0:1], m[1:2]), 0.0)
        xx = _grid_shift(h, h_up, h_down, grid_w) - h
    else:
        xx = _line_shift(h) - h
    mu = mu_ref[...]
    s1 = s1_ref[...]
    s2 = s2_ref[...]

    def mix(j):
        return (h + xx * mu[j:j + 1]).astype(BF16)

    r = _dot(mix(0), wr_ref[...])
    k = _dot(mix(2), wk_ref[...])
    xv = mix(3)
    v = _dot(xv, wv_ref[...])
    if has_vres:
        lo = _dot(_dot(xv, v1_ref[...]).astype(BF16), v2_ref[...])
        v = v + (vf_ref[...] - v) * jax.nn.sigmoid(v0_ref[...] + lo)
    kk = k * kk_ref[...]
    kk = kk * lax.rsqrt(jnp.maximum(_head_sum(kk * kk, s1, s2), L2_EPS))
    wl = jnp.tanh(_dot(mix(1), w1_ref[...])).astype(BF16)
    wd = w0_ref[...] + _dot(wl, w2_ref[...])
    al = _dot(mix(4), a1_ref[...]).astype(BF16)
    iclr2 = jax.nn.sigmoid(a0_ref[...] + _dot(al, a2_ref[...]))
    g = _dot(jax.nn.sigmoid(_dot(mix(5), g1_ref[...])).astype(BF16), g2_ref[...])
    ka = ka_ref[...]
    rk = rk_ref[...]
    a = -kk
    v_o[...] = v
    g_o[...] = g
    a_o[...] = a
    r_o[...] = r
    bonus = jnp.zeros_like(r)
    for dr in range(2):
        iclr = iclr2[:, dr * d:(dr + 1) * d]
        kd = k * (1.0 + (iclr - 1.0) * ka)
        lw_o[dr] = jax.nn.sigmoid(wd[:, dr * d:(dr + 1) * d]) * DECAY_SCALE
        k_o[dr] = kd
        b_o[dr] = kk * iclr
        bonus = bonus + r * kd * rk
    bonus_o[...] = _head_sum(bonus, s1, s2) * v


def _rwkv_proj(st, x, norm_g, mods, vfirst, p, grid_w, tm=TOKEN_TILE):
    has_vres = vfirst is not None
    tok = st.tok_spec(tm)
    hw = grid_w or CHUNK
    assert tm % hw == 0 and (grid_w or tm == st.l)
    rpt = tm // hw
    last = st.l // hw - 1
    up_spec = pl.BlockSpec((None, hw, st.d), lambda b, i: (b, jnp.maximum(i * rpt - 1, 0), 0))
    down_spec = pl.BlockSpec((None, hw, st.d), lambda b, i: (b, jnp.minimum((i + 1) * rpt, last), 0))
    ins = [x, x, x, norm_g.reshape(1, st.d), mods] + ([vfirst] if has_vres else [])
    specs = [tok, up_spec, down_spec, _const_spec((1, st.d)), st.mod_spec()] + ([tok] if has_vres else [])
    consts = [p['mu'], p['wr'], p['wk'], p['wv'], p['w1'], p['w2'], p['w0'],
              p['a1'], p['a2'], p['a0'], p['g1'], p['g2'], p['kk'], p['ka'], p['rk'],
              p['s1'], p['s2']]
    if has_vres:
        consts += [p['v0'], p['v1'], p['v2']]
    ins += consts
    specs += [_const_spec(c.shape) for c in consts]
    return pl.pallas_call(
        functools.partial(_rwkv_proj_kernel, has_vres, grid_w),
        grid=st.grid(tm),
        in_specs=specs,
        out_specs=[tok] * 5 + [st.dir_spec(tm)] * 3,
        out_shape=[st.shape()] * 5 + [st.dir_shape()] * 3,
        compiler_params=_cparams(_PAR2),
        name="rwkv_proj",
    )(*ins)


def _dot_nt(a, b):
    return lax.dot_general(a, b, (((1,), (1,)), ((), ())), preferred_element_type=F32)


def _wkv_chunk_kernel(nb, a_ref, r_ref, v_ref, lw_ref, k_ref, b_ref, s0_ref, m4_ref,
                      y_ref, sf_ref, s_ref):
    dr = pl.program_id(0)
    ch = pl.program_id(1)
    ngroup = a_ref.shape[-1] // GROUP
    rev = dr == 1

    @pl.when(ch == 0)
    def _():
        s_ref[...] = s0_ref[...]

    m4 = m4_ref[...]
    row = lax.broadcasted_iota(jnp.int32, (CHUNK, GROUP), 0)
    col = lax.broadcasted_iota(jnp.int32, (CHUNK, GROUP), 1) % HEAD_DIM
    ahead = jnp.where(rev, row - col, col - row)
    strict = (ahead > 0).astype(F32)
    incl = (ahead >= 0).astype(F32)

    def bd(x):
        xb = x.astype(BF16)
        return jnp.concatenate([xb, xb, xb, xb], axis=0) * m4

    def head_t(x):
        t = x.T
        return jnp.concatenate(
            [t[h * HEAD_DIM:(h + 1) * HEAD_DIM, :] for h in range(GROUP // HEAD_DIM)], axis=1)

    def run_sum(x):
        p = x
        shift = 1
        while shift < CHUNK:
            p = p + jnp.where(row >= shift, pltpu.roll(p, shift, axis=0), 0.0)
            shift *= 2
        return jnp.where(rev, p[CHUNK - 1:CHUNK] - p + x, p)

    def body(it, carry):
        gs = range(ngroup * WKV_NB)
        bis = [it * WKV_NB + g // ngroup for g in gs]
        gi = [g % ngroup for g in gs]
        sl = [slice(gi[g] * GROUP, (gi[g] + 1) * GROUP) for g in gs]
        a = [a_ref[bis[g], :, sl[g]] for g in gs]
        r = [r_ref[bis[g], :, sl[g]] for g in gs]
        v = [v_ref[bis[g], :, sl[g]] for g in gs]
        lw = [lw_ref[bis[g], :, sl[g]] for g in gs]
        k = [k_ref[bis[g], :, sl[g]] for g in gs]
        b = [b_ref[bis[g], :, sl[g]] for g in gs]
        s0 = [s_ref[bis[g], gi[g]] for g in gs]
        cum = [run_sum(lw[g]) for g in gs]
        at = [a[g] * jnp.exp(cum[g] - lw[g]) for g in gs]
        rt = [r[g] * jnp.exp(cum[g]) for g in gs]
        ginv = [jnp.exp(-cum[g]) for g in gs]
        bt = [b[g] * ginv[g] for g in gs]
        kt = [k[g] * ginv[g] for g in gs]
        gtot = [jnp.exp(jnp.where(rev, cum[g][0:1], cum[g][CHUNK - 1:CHUNK])) for g in gs]
        z = [jnp.concatenate([bt[g], kt[g], s0[g]], axis=0).astype(BF16) for g in gs]
        ma = [_dot_nt(z[g], bd(at[g])) for g in gs]
        mr = [_dot_nt(z[g], bd(rt[g])) for g in gs]
        vt = [head_t(v[g]) for g in gs]
        q = [ma[g][0:CHUNK] * strict for g in gs]
        u = [ma[g][2 * CHUNK:] + _dot(vt[g].astype(BF16), bd(ma[g][CHUNK:2 * CHUNK] * strict)) for g in gs]
        for j in range(6):
            for g in gs:
                if j < 5:
                    p = _dot(jnp.concatenate([u[g], q[g]], axis=0).astype(BF16), bd(q[g]))
                    u[g] = u[g] + p[0:CHUNK]
                    q[g] = p[CHUNK:]
                else:
                    u[g] = u[g] + _dot(u[g].astype(BF16), bd(q[g]))
        uv = [jnp.concatenate([u[g], vt[g]], axis=1).astype(BF16) for g in gs]
        yt = [mr[g][2 * CHUNK:] + _dot(uv[g], jnp.concatenate(
            [bd(mr[g][0:CHUNK] * incl), bd(mr[g][CHUNK:2 * CHUNK] * incl)], axis=0)) for g in gs]
        sn = [(s0[g] + _dot(uv[g], jnp.concatenate([bd(bt[g]), bd(kt[g])], axis=0))) * gtot[g]
              for g in gs]
        for g in gs:
            s_ref[bis[g], gi[g]] = sn[g]
            y_ref[bis[g], :, sl[g]] = head_t(yt[g])
        return carry

    lax.fori_loop(0, nb // WKV_NB, body, 0)

    @pl.when(ch == pl.num_programs(1) - 1)
    def _():
        sf_ref[...] = s_ref[...]


def _wkv(st, a, r, v, lw, k, b, s0):
    nb, l, d = st.b, st.l, st.d
    ngroup = d // GROUP
    nch = l // CHUNK
    assert l % CHUNK == 0 and d % GROUP == 0 and nb % WKV_NB == 0

    def chunk(dr, j):
        return jnp.where(dr == 1, nch - 1 - j, j)

    sh_spec = pl.BlockSpec((nb, CHUNK, d), lambda dr, j: (0, chunk(dr, j), 0))
    dir_spec = pl.BlockSpec((None, nb, CHUNK, d), lambda dr, j: (dr, 0, chunk(dr, j), 0))
    st_spec = pl.BlockSpec((None, nb, ngroup, HEAD_DIM, GROUP), lambda dr, j: (dr, 0, 0, 0, 0))
    blk = jnp.arange(GROUP) // HEAD_DIM
    m4 = (blk[:, None] == blk[None, :]).astype(BF16)
    return pl.pallas_call(
        functools.partial(_wkv_chunk_kernel, nb),
        grid=(2, nch),
        in_specs=[sh_spec, sh_spec, sh_spec, dir_spec, dir_spec, dir_spec, st_spec,
                  _const_spec(m4.shape)],
        out_specs=[dir_spec, st_spec],
        out_shape=[jax.ShapeDtypeStruct((2, nb, l, d), F32), jax.ShapeDtypeStruct(s0.shape, F32)],
        scratch_shapes=[pltpu.VMEM((nb, ngroup, HEAD_DIM, GROUP), F32)],
        compiler_params=_cparams(("arbitrary", "arbitrary")),
        name="wkv",
    )(a, r, v, lw, k, b, s0, m4)


def _mlp_rows(x, m, mlp, final_ref=None):
    g_ref, w1_ref, w2_ref = mlp
    hn = _norm_mod(x, g_ref[...], m[3:4], m[4:5]).astype(BF16)
    d = x.shape[-1]
    f = w1_ref.shape[-1]
    acc = jnp.zeros_like(x)
    for c in range(f // d):
        hid = jnp.maximum(_dot(hn, w1_ref[:, c * d:(c + 1) * d]), 0.0)
        acc = acc + _dot((hid * hid).astype(BF16), w2_ref[c * d:(c + 1) * d, :])
    y = x + m[5:6] * acc
    if final_ref is not None:
        ms = jnp.mean(y * y, axis=-1, keepdims=True)
        y = (y * lax.rsqrt(ms + RMS_EPS)) * final_ref[...]
    return y


def _mlp_consts(st, g, w1, w2, final_g):
    ops = [g.reshape(1, st.d), w1, w2] + ([final_g.reshape(1, st.d)] if final_g is not None else [])
    return ops, [_const_spec(o.shape) for o in ops]


def _mlp_kernel(x_ref, m_ref, g_ref, w1_ref, w2_ref, o_ref):
    o_ref[...] = _mlp_rows(x_ref[...], m_ref[...], (g_ref, w1_ref, w2_ref))


def _mlp(st, x, mods, g, w1, w2, tm=TOKEN_TILE):
    tm = min(tm, st.l)
    ops, specs = _mlp_consts(st, g, w1, w2, None)
    return pl.pallas_call(
        _mlp_kernel,
        grid=st.grid(tm),
        in_specs=[st.tok_spec(tm), st.mod_spec()] + specs,
        out_specs=st.tok_spec(tm),
        out_shape=st.shape(),
        compiler_params=_cparams(_PAR2),
        name="mlp",
    )(x, mods, *ops)


def _rwkv_out_kernel(final, x_ref, y_ref, bonus_ref, g_ref, m_ref, lw_ref, lb_ref, wo_ref,
                     s1_ref, s2_ref, *rest):
    mlp, final_ref, o_ref = rest[:3], (rest[3] if final else None), rest[-1]
    s1 = s1_ref[...]
    s2 = s2_ref[...]
    m = m_ref[...]
    y = y_ref[0] + y_ref[1]
    mean = _head_sum(y, s1, s2) * (1.0 / HEAD_DIM)
    yc = y - mean
    var = _head_sum(yc * yc, s1, s2) * (1.0 / HEAD_DIM)
    yn = (yc * lax.rsqrt(var + LNX_EPS)) * lw_ref[...] + lb_ref[...]
    out = (yn + bonus_ref[...]) * g_ref[...]
    o = _dot(out.astype(BF16), wo_ref[...])
    o_ref[...] = _mlp_rows(x_ref[...] + m[2:3] * o, m, mlp, final_ref)


def _rwkv_out(st, x, y, bonus, g, mods, p, mlp_g, w1, w2, final_g=None, tm=TOKEN_TILE):
    tok = st.tok_spec(tm)
    consts = [p['lnx_w'], p['lnx_b'], p['wo'], p['s1'], p['s2']]
    ops, specs = _mlp_consts(st, mlp_g, w1, w2, final_g)
    return pl.pallas_call(
        functools.partial(_rwkv_out_kernel, final_g is not None),
        grid=st.grid(tm),
        in_specs=[tok, st.dir_spec(tm), tok, tok, st.mod_spec()]
        + [_const_spec(c.shape) for c in consts] + specs,
        out_specs=tok,
        out_shape=st.shape(),
        compiler_params=_cparams(_PAR2),
        name="rwkv_out_mlp",
    )(x, y, bonus, g, mods, *consts, *ops)


def _dft_mats(n):
    idx = jnp.arange(n, dtype=jnp.int32)
    ang = ((idx[:, None] * idx[None, :]) % n).astype(F32) * (2.0 * jnp.pi / n)
    return jnp.cos(ang), jnp.sin(ang)


def _fnet_chan_kernel(x_ref, g_ref, m_ref, c_ref, s_ref, o_ref):
    m = m_ref[...]
    d = x_ref.shape[-1]
    cg = d // FOURIER_GROUPS
    h = _norm_mod(x_ref[...], g_ref[...], m[0:1], m[1:2]).astype(BF16)
    for gi in range(FOURIER_GROUPS):
        hg = h[:, gi * cg:(gi + 1) * cg]
        o_ref[:, gi * cg:(gi + 1) * cg] = _dot(hg, c_ref[...]).astype(BF16)
        o_ref[:, d + gi * cg:d + (gi + 1) * cg] = _dot(hg, s_ref[...]).astype(BF16)


def _fnet_chan(st, x, g, mods, cc, sc, tm=TOKEN_TILE):
    return pl.pallas_call(
        _fnet_chan_kernel,
        grid=st.grid(tm),
        in_specs=[st.tok_spec(tm), _const_spec((1, st.d)), st.mod_spec(),
                  _const_spec(cc.shape), _const_spec(sc.shape)],
        out_specs=st.tok_spec(tm, 2 * st.d),
        out_shape=st.shape(2 * st.d, BF16),
        compiler_params=_cparams(_PAR2),
        name="fnet_chan",
    )(x, g.reshape(1, st.d), mods, cc, sc)


def _fnet_seq_kernel(scale, x_ref, xcs_ref, cl_ref, sl_ref, m_ref, wo_ref, o_ref, acc_ref):
    kstep = pl.program_id(2)
    d = x_ref.shape[-1]

    @pl.when(kstep == 0)
    def _():
        acc_ref[...] = jnp.zeros_like(acc_ref)

    acc_ref[...] += _dot(cl_ref[...], xcs_ref[:, :d]) - _dot(sl_ref[...], xcs_ref[:, d:])

    @pl.when(kstep == pl.num_programs(2) - 1)
    def _():
        f = (acc_ref[...] * scale).astype(BF16)
        o_ref[...] = x_ref[...] + m_ref[...][2:3] * _dot(f, wo_ref[...])


def _fnet_seq(st, x, xcs, mods, wo, tl=DFT_OUT_TILE, tk=DFT_K_TILE):
    d, l = st.d, st.l
    tl = min(tl, l)
    tk = min(tk, l)
    cl, sl = _dft_mats(l)
    cl = cl.astype(BF16)
    sl = sl.astype(BF16)
    scale = 1.0 / float(l * (d // FOURIER_GROUPS)) ** 0.5
    return pl.pallas_call(
        functools.partial(_fnet_seq_kernel, scale),
        grid=(st.b, l // tl, l // tk),
        in_specs=[
            pl.BlockSpec((None, tl, d), lambda b, i, k: (b, i, 0)),
            pl.BlockSpec((None, tk, 2 * d), lambda b, i, k: (b, k, 0)),
            pl.BlockSpec((tl, tk), lambda b, i, k: (i, k)),
            pl.BlockSpec((tl, tk), lambda b, i, k: (i, k)),
            st.mod_spec(),
            _const_spec(wo.shape),
        ],
        out_specs=pl.BlockSpec((None, tl, d), lambda b, i, k: (b, i, 0)),
        out_shape=st.shape(),
        scratch_shapes=[pltpu.VMEM((tl, d), F32)],
        compiler_params=_cparams(("parallel", "parallel", "arbitrary")),
        name="fnet_seq",
    )(x, xcs, cl, sl, mods, wo)


def _fft1_kernel(x_ref, g_ref, m_ref, cc_ref, sc_ref, w1_ref, tc_ref, ts_ref, o_ref):
    n, j_n, d = x_ref.shape
    cg = d // FOURIER_GROUPS
    m = m_ref[...]
    xs = jnp.concatenate([x_ref[:, j, :] for j in range(j_n)], axis=0)
    h = _norm_mod(xs, g_ref[...], m[0:1], m[1:2]).astype(BF16)
    xc = []
    xsn = []
    for gi in range(FOURIER_GROUPS):
        hg = h[:, gi * cg:(gi + 1) * cg]
        xc.append(_dot(hg, cc_ref[...]).astype(BF16))
        xsn.append(_dot(hg, sc_ref[...]).astype(BF16))
    xcs = jnp.concatenate(xc + xsn, axis=1)
    w1 = w1_ref[...]
    for j in range(j_n):
        p = _dot(w1, xcs[j * n:(j + 1) * n])
        a_re = p[:n, :d] - p[n:, d:]
        a_im = -(p[:n, d:] + p[n:, :d])
        tc = jnp.concatenate([tc_ref[j]] * (d // LANES), axis=1)
        ts = jnp.concatenate([ts_ref[j]] * (d // LANES), axis=1)
        o_ref[:, j, :d] = (a_re * tc + a_im * ts).astype(BF16)
        o_ref[:, j, d:] = (a_im * tc - a_re * ts).astype(BF16)


def _fft2_kernel(scale, final, x_ref, b_ref, m_ref, w2_ref, wo_ref, *rest):
    mlp, final_ref, o_ref = rest[:3], (rest[3] if final else None), rest[-1]
    n, j_n, d = x_ref.shape
    w2 = w2_ref[...]
    m = m_ref[...]
    f = []
    for j in range(j_n):
        bj = b_ref[j]
        f.append(_dot(w2, jnp.concatenate([bj[:, :d], bj[:, d:]], axis=0)))
    fa = (jnp.concatenate(f, axis=0) * scale).astype(BF16)
    xs = jnp.concatenate([x_ref[:, j, :] for j in range(j_n)], axis=0)
    y = _mlp_rows(xs + m[2:3] * _dot(fa, wo_ref[...]), m, mlp, final_ref)
    for j in range(j_n):
        o_ref[:, j, :] = y[j * n:(j + 1) * n]


def _fnet_fft(st, x, g, mods, cc, sc, wo, mlp_g, mlp_w1, mlp_w2, final_g=None):
    nb, l, d = st.b, st.l, st.d
    n, jb = FFT_N, FFT_J
    assert l == n * n
    c1, s1 = _dft_mats(n)
    w1 = jnp.concatenate([c1, s1], axis=0).astype(BF16)
    w2 = jnp.concatenate([c1, s1], axis=1).astype(BF16)
    idx = jnp.arange(n, dtype=jnp.int32)
    ang = (idx[:, None] * idx[None, :]).astype(F32) * (2.0 * jnp.pi / l)
    tc = jnp.broadcast_to(jnp.cos(ang)[:, :, None], (n, n, LANES))
    ts = jnp.broadcast_to(jnp.sin(ang)[:, :, None], (n, n, LANES))
    scale = 1.0 / float(l * (d // FOURIER_GROUPS)) ** 0.5
    x4 = x.reshape(nb, n, n, d)
    blk = pl.BlockSpec((None, n, jb, d), lambda b, i: (b, 0, i, 0))
    tw = pl.BlockSpec((jb, n, LANES), lambda b, i: (i, 0, 0))
    bmat = pl.pallas_call(
        _fft1_kernel,
        grid=(nb, n // jb),
        in_specs=[blk, _const_spec((1, d)), st.mod_spec(), _const_spec(cc.shape), _const_spec(sc.shape),
                  _const_spec(w1.shape), tw, tw],
        out_specs=pl.BlockSpec((None, n, jb, 2 * d), lambda b, i: (b, 0, i, 0)),
        out_shape=jax.ShapeDtypeStruct((nb, n, n, 2 * d), BF16),
        compiler_params=_cparams(_PAR2),
        name="fnet_fft1",
    )(x4, g.reshape(1, d), mods, cc, sc, w1, tc, ts)
    ops, specs = _mlp_consts(st, mlp_g, mlp_w1, mlp_w2, final_g)
    out = pl.pallas_call(
        functools.partial(_fft2_kernel, scale, final_g is not None),
        grid=(nb, n // jb),
        in_specs=[blk, pl.BlockSpec((None, jb, n, 2 * d), lambda b, i: (b, i, 0, 0)),
                  st.mod_spec(), _const_spec(w2.shape), _const_spec(wo.shape)] + specs,
        out_specs=blk,
        out_shape=jax.ShapeDtypeStruct((nb, n, n, d), F32),
        compiler_params=_cparams(_PAR2),
        name="fnet_fft2_mlp",
    )(x4, bmat, mods, w2, wo, *ops)
    return out.reshape(nb, l, d)


def _blockdiag2(w):
    z = jnp.zeros_like(w[0])
    return jnp.concatenate([jnp.concatenate([w[0], z], axis=1),
                            jnp.concatenate([z, w[1]], axis=1)], axis=0)


def kernel(x, c, ctx, c_ctx, norm1_g, norm2_g, mod_w, mod_b, mlp_w1, mlp_w2, rk_mu, rk_wr, rk_wk, rk_wv, rk_wo, rk_w0, rk_w1, rk_w2, rk_a0, rk_a1, rk_a2, rk_v0, rk_v1, rk_v2, rk_g1, rk_g2, rk_kk, rk_ka, rk_rk, rk_lnx_w, rk_lnx_b, ft_wo, final_g):
    nb, seq, d = x.shape
    ctx_len = ctx.shape[1]
    depth = mod_w.shape[0]
    assert nb + 1 <= MOD_ROWS and d % LANES == 0 and seq == FFT_N * FFT_N
    lat = _Stream(nb, seq, d)
    cst = _Stream(nb, ctx_len, d, ctx_row=nb)
    row = lambda t: t.reshape(1, -1)

    cvec = jnp.zeros((MOD_ROWS, d), F32).at[:nb].set(c).at[nb].set(c_ctx)
    mods_all = _mod_table(cvec, mod_w, mod_b).reshape(depth, MOD_ROWS, N_MOD, d)

    heads = jnp.arange(d) // HEAD_DIM
    s1 = (heads[:, None] == jnp.arange(LANES)[None, :]).astype(BF16)
    s2 = s1.T
    cg = d // FOURIER_GROUPS
    cc, sc = _dft_mats(cg)
    cc = cc.astype(BF16)
    sc = sc.astype(BF16)
    zero_state = jnp.zeros((2, nb, d // GROUP, HEAD_DIM, GROUP), F32)

    last_a = ((depth - 1) // 2) * 2
    vf_lat = vf_ctx = None
    for i in range(depth):
        idx = i // 2
        mods = mods_all[i]
        ctx_out = i < last_a
        w1 = mlp_w1[i].astype(BF16)
        w2 = mlp_w2[i].astype(BF16)
        fin = final_g if i == depth - 1 else None
        if i % 2 == 0:
            p = {
                'mu': rk_mu[idx], 'wr': rk_wr[idx].astype(BF16), 'wk': rk_wk[idx].astype(BF16),
                'wv': rk_wv[idx].astype(BF16), 'wo': rk_wo[idx].astype(BF16),
                'w1': jnp.concatenate([rk_w1[idx, 0], rk_w1[idx, 1]], axis=1).astype(BF16),
                'w2': _blockdiag2(rk_w2[idx]).astype(BF16), 'w0': row(rk_w0[idx]),
                'a1': jnp.concatenate([rk_a1[idx, 0], rk_a1[idx, 1]], axis=1).astype(BF16),
                'a2': _blockdiag2(rk_a2[idx]).astype(BF16), 'a0': row(rk_a0[idx]),
                'g1': rk_g1[idx].astype(BF16), 'g2': rk_g2[idx].astype(BF16),
                'kk': row(rk_kk[idx]), 'ka': row(rk_ka[idx]), 'rk': row(rk_rk[idx]),
                'lnx_w': row(rk_lnx_w[idx]), 'lnx_b': row(rk_lnx_b[idx]), 's1': s1, 's2': s2,
            }
            if idx > 0:
                p['v0'] = row(rk_v0[idx - 1])
                p['v1'] = rk_v1[idx - 1].astype(BF16)
                p['v2'] = rk_v2[idx - 1].astype(BF16)
            v_c, g_c, bonus_c, a_c, r_c, lw_c, k_c, b_c = _rwkv_proj(
                cst, ctx, norm1_g[i], mods, vf_ctx if idx > 0 else None, p, 0, tm=ctx_len)
            y_c, s_ctx = _wkv(cst, a_c, r_c, v_c, lw_c, k_c, b_c, zero_state)
            v_l, g_l, bonus_l, a_l, r_l, lw_l, k_l, b_l = _rwkv_proj(
                lat, x, norm1_g[i], mods, vf_lat if idx > 0 else None, p, GRID_W)
            y_l, _ = _wkv(lat, a_l, r_l, v_l, lw_l, k_l, b_l, s_ctx)
            if idx == 0:
                vf_ctx, vf_lat = v_c, v_l
            x = _rwkv_out(lat, x, y_l, bonus_l, g_l, mods, p, norm2_g[i], w1, w2, fin)
            if ctx_out:
                ctx = _rwkv_out(cst, ctx, y_c, bonus_c, g_c, mods, p, norm2_g[i], w1, w2)
        else:
            wo = ft_wo[idx].astype(BF16)
            x = _fnet_fft(lat, x, norm1_g[i], mods, cc, sc, wo, norm2_g[i], w1, w2, fin)
            if ctx_out:
                ctx = _fnet_seq(cst, ctx, _fnet_chan(cst, ctx, norm1_g[i], mods, cc, sc), mods, wo)
                ctx = _mlp(cst, ctx, mods, norm2_g[i], w1, w2)
    return x
```

```python
import functools

import jax
import jax.numpy as jnp
from jax import lax
from jax.experimental import pallas as pl
from jax.experimental.pallas import tpu as pltpu

HEAD_DIM = 64
GRID_W = 64
N_MOD = 6
FOURIER_GROUPS = 4
RMS_EPS = 1e-6
LNX_EPS = 64e-5
L2_EPS = 1e-24
DECAY_SCALE = -0.6065306597126334

MOD_ROWS = 16
LANES = 128
GROUP = 256
CHUNK = 64
WKV_NB = 4
FFT_N = 64
FFT_J = 8

VMEM_LIMIT = 56 * 1024 * 1024
TOKEN_TILE = 256
DFT_OUT_TILE = 1024
DFT_K_TILE = 512

F32 = jnp.float32
BF16 = jnp.bfloat16


def _cparams(sem):
    return pltpu.CompilerParams(dimension_semantics=sem, vmem_limit_bytes=VMEM_LIMIT)


def _const_spec(shape):
    nd = len(shape)
    return pl.BlockSpec(shape, lambda *_: (0,) * nd, pipeline_mode=pl.Buffered(1))


def _dot(a, b):
    return jnp.dot(a, b, preferred_element_type=F32)


def _split2(x):
    hi = x.astype(BF16)
    lo = (x - hi.astype(F32)).astype(BF16)
    return hi, lo


def _head_sum(x, s1, s2):
    s = _dot(x.astype(BF16), s1)
    hi, lo = _split2(s)
    return _dot(hi, s2) + _dot(lo, s2)


def _norm_mod(x, g, sh, sc):
    ms = jnp.mean(x * x, axis=-1, keepdims=True)
    return (x * lax.rsqrt(ms + RMS_EPS)) * g * (1.0 + sc) + sh


def _mod_kernel(c_ref, w_ref, b_ref, o_ref):
    c = c_ref[...]
    s = c * jax.nn.sigmoid(c)
    o_ref[...] = jnp.dot(s, w_ref[...], precision=lax.Precision.HIGHEST,
                         preferred_element_type=F32) + b_ref[...]


def _mod_table(cvec, mod_w, mod_b):
    depth, d, nd = mod_w.shape
    nt = nd // d
    return pl.pallas_call(
        _mod_kernel,
        grid=(depth, nt),
        in_specs=[
            pl.BlockSpec((MOD_ROWS, d), lambda l, j: (0, 0)),
            pl.BlockSpec((None, d, d), lambda l, j: (l, 0, j)),
            pl.BlockSpec((None, 1, d), lambda l, j: (l, 0, j)),
        ],
        out_specs=pl.BlockSpec((None, MOD_ROWS, d), lambda l, j: (l, 0, j)),
        out_shape=jax.ShapeDtypeStruct((depth, MOD_ROWS, nd), F32),
        compiler_params=_cparams(("arbitrary", "arbitrary")),
        name="mod_table",
    )(cvec, mod_w, mod_b.reshape(depth, 1, nd))


class _Stream:
    def __init__(self, b, l, d, ctx_row=None):
        self.b, self.l, self.d, self.ctx_row = b, l, d, ctx_row

    def grid(self, tm):
        assert self.l % tm == 0
        return (self.b, self.l // tm)

    def tok_spec(self, tm, width=None):
        return pl.BlockSpec((None, tm, width or self.d), lambda b, i: (b, i, 0))

    def dir_spec(self, tm):
        return pl.BlockSpec((2, None, tm, self.d), lambda b, i: (0, b, i, 0))

    def dir_shape(self, dtype=F32):
        return jax.ShapeDtypeStruct((2, self.b, self.l, self.d), dtype)

    def mod_row(self, b):
        return b if self.ctx_row is None else b * 0 + self.ctx_row

    def mod_spec(self):
        return pl.BlockSpec((None, N_MOD, self.d), lambda b, *_: (self.mod_row(b), 0, 0))

    def shape(self, width=None, dtype=F32):
        return jax.ShapeDtypeStruct((self.b, self.l, width or self.d), dtype)


_PAR2 = ("parallel", "parallel")


def _grid_shift(h, h_up, h_down, grid_w):
    tm, d = h.shape
    q = d // 4
    col = lax.broadcasted_iota(jnp.int32, (tm, q), 0) % grid_w
    left = jnp.where(col == 0, 0.0, pltpu.roll(h[:, 0:q], 1, axis=0))
    right = jnp.where(col == grid_w - 1, 0.0, pltpu.roll(h[:, q:2 * q], tm - 1, axis=0))
    up = jnp.concatenate([h_up[:, 2 * q:3 * q], h[:tm - grid_w, 2 * q:3 * q]], axis=0)
    down = jnp.concatenate([h[grid_w:, 3 * q:], h_down[:, 3 * q:]], axis=0)
    return jnp.concatenate([left, right, up, down], axis=1)


def _line_shift(h):
    tm, d = h.shape
    half = d // 2
    t = lax.broadcasted_iota(jnp.int32, (tm, half), 0)
    prev = jnp.where(t == 0, 0.0, pltpu.roll(h[:, :half], 1, axis=0))
    nxt = jnp.where(t == tm - 1, 0.0, pltpu.roll(h[:, half:], tm - 1, axis=0))
    return jnp.concatenate([prev, nxt], axis=1)


def _rwkv_proj_kernel(has_vres, grid_w, *refs):
    if has_vres:
        (x_ref, xu_ref, xd_ref, ng_ref, m_ref, vf_ref, mu_ref, wr_ref, wk_ref, wv_ref, w1_ref, w2_ref, w0_ref,
         a1_ref, a2_ref, a0_ref, g1_ref, g2_ref, kk_ref, ka_ref, rk_ref, s1_ref, s2_ref,
         v0_ref, v1_ref, v2_ref,
         v_o, g_o, bonus_o, a_o, r_o, lw_o, k_o, b_o) = refs
    else:
        (x_ref, xu_ref, xd_ref, ng_ref, m_ref, mu_ref, wr_ref, wk_ref, wv_ref, w1_ref, w2_ref, w0_ref,
         a1_ref, a2_ref, a0_ref, g1_ref, g2_ref, kk_ref, ka_ref, rk_ref, s1_ref, s2_ref,
         v_o, g_o, bonus_o, a_o, r_o, lw_o, k_o, b_o) = refs
    d = x_ref.shape[-1]
    m = m_ref[...]
    ng = ng_ref[...]
    h = _norm_mod(x_ref[...], ng, m[0:1], m[1:2])
    if grid_w:
        i = pl.program_id(1)
        h_up = jnp.where(i > 0, _norm_mod(xu_ref[...], ng, m[0:1], m[1:2]), 0.0)
        h_down = jnp.where(i < pl.num_programs(1) - 1, _norm_mod(xd_ref[...], ng, m[0:1], m[1:2]), 0.0)
        xx = _grid_shift(h, h_up, h_down, grid_w) - h
    else:
        xx = _line_shift(h) - h
    mu = mu_ref[...]
    s1 = s1_ref[...]
    s2 = s2_ref[...]

    def mix(j):
        return (h + xx * mu[j:j + 1]).astype(BF16)

    r = _dot(mix(0), wr_ref[...])
    k = _dot(mix(2), wk_ref[...])
    xv = mix(3)
    v = _dot(xv, wv_ref[...])
    if has_vres:
        lo = _dot(_dot(xv, v1_ref[...]).astype(BF16), v2_ref[...])
        v = v + (vf_ref[...] - v) * jax.nn.sigmoid(v0_ref[...] + lo)
    kk = k * kk_ref[...]
    kk = kk * lax.rsqrt(jnp.maximum(_head_sum(kk * kk, s1, s2), L2_EPS))
    wl = jnp.tanh(_dot(mix(1), w1_ref[...])).astype(BF16)
    wd = w0_ref[...] + _dot(wl, w2_ref[...])
    al = _dot(mix(4), a1_ref[...]).astype(BF16)
    iclr2 = jax.nn.sigmoid(a0_ref[...] + _dot(al, a2_ref[...]))
    g = _dot(jax.nn.sigmoid(_dot(mix(5), g1_ref[...])).astype(BF16), g2_ref[...])
    ka = ka_ref[...]
    rk = rk_ref[...]
    a = -kk
    v_o[...] = v
    g_o[...] = g
    a_o[...] = a.astype(BF16)
    r_o[...] = r.astype(BF16)
    bonus = jnp.zeros_like(r)
    for dr in range(2):
        iclr = iclr2[:, dr * d:(dr + 1) * d]
        kd = k * (1.0 + (iclr - 1.0) * ka)
        lw_o[dr] = jax.nn.sigmoid(wd[:, dr * d:(dr + 1) * d]) * DECAY_SCALE
        k_o[dr] = kd.astype(BF16)
        b_o[dr] = (kk * iclr).astype(BF16)
        bonus = bonus + r * kd * rk
    bonus_o[...] = _head_sum(bonus, s1, s2) * v


def _rwkv_proj(st, x, norm_g, mods, vfirst, p, grid_w, tm=TOKEN_TILE):
    has_vres = vfirst is not None
    tok = st.tok_spec(tm)
    hw = grid_w or CHUNK
    assert tm % hw == 0 and (grid_w or tm == st.l)
    rpt = tm // hw
    last = st.l // hw - 1
    up_spec = pl.BlockSpec((None, hw, st.d), lambda b, i: (b, jnp.maximum(i * rpt - 1, 0), 0))
    down_spec = pl.BlockSpec((None, hw, st.d), lambda b, i: (b, jnp.minimum((i + 1) * rpt, last), 0))
    ins = [x, x, x, norm_g.reshape(1, st.d), mods] + ([vfirst] if has_vres else [])
    specs = [tok, up_spec, down_spec, _const_spec((1, st.d)), st.mod_spec()] + ([tok] if has_vres else [])
    consts = [p['mu'], p['wr'], p['wk'], p['wv'], p['w1'], p['w2'], p['w0'],
              p['a1'], p['a2'], p['a0'], p['g1'], p['g2'], p['kk'], p['ka'], p['rk'],
              p['s1'], p['s2']]
    if has_vres:
        consts += [p['v0'], p['v1'], p['v2']]
    ins += consts
    specs += [_const_spec(c.shape) for c in consts]
    return pl.pallas_call(
        functools.partial(_rwkv_proj_kernel, has_vres, grid_w),
        grid=st.grid(tm),
        in_specs=specs,
        out_specs=[tok] * 5 + [st.dir_spec(tm)] * 3,
        out_shape=[st.shape()] * 3 + [st.shape(dtype=BF16)] * 2
        + [st.dir_shape(), st.dir_shape(BF16), st.dir_shape(BF16)],
        compiler_params=_cparams(_PAR2),
        name="rwkv_proj",
    )(*ins)


def _dot_nt(a, b):
    return lax.dot_general(a, b, (((1,), (1,)), ((), ())), preferred_element_type=F32)


def _wkv_chunk_kernel(nb, a_ref, r_ref, v_ref, lw_ref, k_ref, b_ref, s0_ref, m4_ref,
                      y_ref, sf_ref, s_ref):
    dr = pl.program_id(0)
    ch = pl.program_id(1)
    ngroup = a_ref.shape[-1] // GROUP
    rev = dr == 1

    @pl.when(ch == 0)
    def _():
        s_ref[...] = s0_ref[...]

    m4 = m4_ref[...]
    row = lax.broadcasted_iota(jnp.int32, (CHUNK, GROUP), 0)
    col = lax.broadcasted_iota(jnp.int32, (CHUNK, GROUP), 1) % HEAD_DIM
    ahead = jnp.where(rev, row - col, col - row)
    strict = (ahead > 0).astype(F32)
    incl = (ahead >= 0).astype(F32)

    def bd(x):
        xb = x.astype(BF16)
        return jnp.concatenate([xb, xb, xb, xb], axis=0) * m4

    def head_t(x):
        t = x.T
        return jnp.concatenate(
            [t[h * HEAD_DIM:(h + 1) * HEAD_DIM, :] for h in range(GROUP // HEAD_DIM)], axis=1)

    def run_sum(x):
        p = x
        shift = 1
        while shift < CHUNK:
            p = p + jnp.where(row >= shift, pltpu.roll(p, shift, axis=0), 0.0)
            shift *= 2
        return jnp.where(rev, p[CHUNK - 1:CHUNK] - p + x, p)

    def body(it, carry):
        gs = range(ngroup * WKV_NB)
        bis = [it * WKV_NB + g // ngroup for g in gs]
        gi = [g % ngroup for g in gs]
        sl = [slice(gi[g] * GROUP, (gi[g] + 1) * GROUP) for g in gs]
        a = [a_ref[bis[g], :, sl[g]].astype(F32) for g in gs]
        r = [r_ref[bis[g], :, sl[g]].astype(F32) for g in gs]
        v = [v_ref[bis[g], :, sl[g]] for g in gs]
        lw = [lw_ref[bis[g], :, sl[g]] for g in gs]
        k = [k_ref[bis[g], :, sl[g]].astype(F32) for g in gs]
        b = [b_ref[bis[g], :, sl[g]].astype(F32) for g in gs]
        s0 = [s_ref[bis[g], gi[g]] for g in gs]
        cum = [run_sum(lw[g]) for g in gs]
        at = [a[g] * jnp.exp(cum[g] - lw[g]) for g in gs]
        rt = [r[g] * jnp.exp(cum[g]) for g in gs]
        ginv = [jnp.exp(-cum[g]) for g in gs]
        bt = [b[g] * ginv[g] for g in gs]
        kt = [k[g] * ginv[g] for g in gs]
        gtot = [jnp.exp(jnp.where(rev, cum[g][0:1], cum[g][CHUNK - 1:CHUNK])) for g in gs]
        z = [jnp.concatenate([bt[g], kt[g], s0[g]], axis=0).astype(BF16) for g in gs]
        ma = [_dot_nt(z[g], bd(at[g])) for g in gs]
        mr = [_dot_nt(z[g], bd(rt[g])) for g in gs]
        vt = [head_t(v[g]) for g in gs]
        q = [ma[g][0:CHUNK] * strict for g in gs]
        u = [ma[g][2 * CHUNK:] + _dot(vt[g].astype(BF16), bd(ma[g][CHUNK:2 * CHUNK] * strict)) for g in gs]
        for j in range(6):
            for g in gs:
                if j < 5:
                    p = _dot(jnp.concatenate([u[g], q[g]], axis=0).astype(BF16), bd(q[g]))
                    u[g] = u[g] + p[0:CHUNK]
                    q[g] = p[CHUNK:]
                else:
                    u[g] = u[g] + _dot(u[g].astype(BF16), bd(q[g]))
        uv = [jnp.concatenate([u[g], vt[g]], axis=1).astype(BF16) for g in gs]
        yt = [mr[g][2 * CHUNK:] + _dot(uv[g], jnp.concatenate(
            [bd(mr[g][0:CHUNK] * incl), bd(mr[g][CHUNK:2 * CHUNK] * incl)], axis=0)) for g in gs]
        sn = [(s0[g] + _dot(uv[g], jnp.concatenate([bd(bt[g]), bd(kt[g])], axis=0))) * gtot[g]
              for g in gs]
        for g in gs:
            s_ref[bis[g], gi[g]] = sn[g]
            y_ref[bis[g], :, sl[g]] = head_t(yt[g])
        return carry

    lax.fori_loop(0, nb // WKV_NB, body, 0)

    @pl.when(ch == pl.num_programs(1) - 1)
    def _():
        sf_ref[...] = s_ref[...]


def _wkv(st, a, r, v, lw, k, b, s0):
    nb, l, d = st.b, st.l, st.d
    ngroup = d // GROUP
    nch = l // CHUNK
    assert l % CHUNK == 0 and d % GROUP == 0 and nb % WKV_NB == 0

    def chunk(dr, j):
        return jnp.where(dr == 1, nch - 1 - j, j)

    sh_spec = pl.BlockSpec((nb, CHUNK, d), lambda dr, j: (0, chunk(dr, j), 0))
    dir_spec = pl.BlockSpec((None, nb, CHUNK, d), lambda dr, j: (dr, 0, chunk(dr, j), 0))
    st_spec = pl.BlockSpec((None, nb, ngroup, HEAD_DIM, GROUP), lambda dr, j: (dr, 0, 0, 0, 0))
    blk = jnp.arange(GROUP) // HEAD_DIM
    m4 = (blk[:, None] == blk[None, :]).astype(BF16)
    return pl.pallas_call(
        functools.partial(_wkv_chunk_kernel, nb),
        grid=(2, nch),
        in_specs=[sh_spec, sh_spec, sh_spec, dir_spec, dir_spec, dir_spec, st_spec,
                  _const_spec(m4.shape)],
        out_specs=[dir_spec, st_spec],
        out_shape=[jax.ShapeDtypeStruct((2, nb, l, d), F32), jax.ShapeDtypeStruct(s0.shape, F32)],
        scratch_shapes=[pltpu.VMEM((nb, ngroup, HEAD_DIM, GROUP), F32)],
        compiler_params=_cparams(("arbitrary", "arbitrary")),
        name="wkv",
    )(a, r, v, lw, k, b, s0, m4)


def _mlp_rows(x, m, mlp, final_ref=None):
    g_ref, w1_ref, w2_ref = mlp
    hn = _norm_mod(x, g_ref[...], m[3:4], m[4:5]).astype(BF16)
    d = x.shape[-1]
    f = w1_ref.shape[-1]
    acc = jnp.zeros_like(x)
    for c in range(f // d):
        hid = jnp.maximum(_dot(hn, w1_ref[:, c * d:(c + 1) * d]), 0.0)
        acc = acc + _dot((hid * hid).astype(BF16), w2_ref[c * d:(c + 1) * d, :])
    y = x + m[5:6] * acc
    if final_ref is not None:
        ms = jnp.mean(y * y, axis=-1, keepdims=True)
        y = (y * lax.rsqrt(ms + RMS_EPS)) * final_ref[...]
    return y


def _mlp_consts(st, g, w1, w2, final_g):
    ops = [g.reshape(1, st.d), w1, w2] + ([final_g.reshape(1, st.d)] if final_g is not None else [])
    return ops, [_const_spec(o.shape) for o in ops]


def _mlp_kernel(x_ref, m_ref, g_ref, w1_ref, w2_ref, o_ref):
    o_ref[...] = _mlp_rows(x_ref[...], m_ref[...], (g_ref, w1_ref, w2_ref))


def _mlp(st, x, mods, g, w1, w2, tm=TOKEN_TILE):
    tm = min(tm, st.l)
    ops, specs = _mlp_consts(st, g, w1, w2, None)
    return pl.pallas_call(
        _mlp_kernel,
        grid=st.grid(tm),
        in_specs=[st.tok_spec(tm), st.mod_spec()] + specs,
        out_specs=st.tok_spec(tm),
        out_shape=st.shape(),
        compiler_params=_cparams(_PAR2),
        name="mlp",
    )(x, mods, *ops)


def _rwkv_out_kernel(final, x_ref, y_ref, bonus_ref, g_ref, m_ref, lw_ref, lb_ref, wo_ref,
                     s1_ref, s2_ref, *rest):
    mlp, final_ref, o_ref = rest[:3], (rest[3] if final else None), rest[-1]
    s1 = s1_ref[...]
    s2 = s2_ref[...]
    m = m_ref[...]
    y = y_ref[0] + y_ref[1]
    mean = _head_sum(y, s1, s2) * (1.0 / HEAD_DIM)
    yc = y - mean
    var = _head_sum(yc * yc, s1, s2) * (1.0 / HEAD_DIM)
    yn = (yc * lax.rsqrt(var + LNX_EPS)) * lw_ref[...] + lb_ref[...]
    out = (yn + bonus_ref[...]) * g_ref[...]
    o = _dot(out.astype(BF16), wo_ref[...])
    o_ref[...] = _mlp_rows(x_ref[...] + m[2:3] * o, m, mlp, final_ref)


def _rwkv_out(st, x, y, bonus, g, mods, p, mlp_g, w1, w2, final_g=None, tm=TOKEN_TILE):
    tok = st.tok_spec(tm)
    consts = [p['lnx_w'], p['lnx_b'], p['wo'], p['s1'], p['s2']]
    ops, specs = _mlp_consts(st, mlp_g, w1, w2, final_g)
    return pl.pallas_call(
        functools.partial(_rwkv_out_kernel, final_g is not None),
        grid=st.grid(tm),
        in_specs=[tok, st.dir_spec(tm), tok, tok, st.mod_spec()]
        + [_const_spec(c.shape) for c in consts] + specs,
        out_specs=tok,
        out_shape=st.shape(),
        compiler_params=_cparams(_PAR2),
        name="rwkv_out_mlp",
    )(x, y, bonus, g, mods, *consts, *ops)


def _dft_mats(n):
    idx = jnp.arange(n, dtype=jnp.int32)
    ang = ((idx[:, None] * idx[None, :]) % n).astype(F32) * (2.0 * jnp.pi / n)
    return jnp.cos(ang), jnp.sin(ang)


def _fnet_chan_kernel(x_ref, g_ref, m_ref, c_ref, s_ref, o_ref):
    m = m_ref[...]
    d = x_ref.shape[-1]
    cg = d // FOURIER_GROUPS
    h = _norm_mod(x_ref[...], g_ref[...], m[0:1], m[1:2]).astype(BF16)
    for gi in range(FOURIER_GROUPS):
        hg = h[:, gi * cg:(gi + 1) * cg]
        o_ref[:, gi * cg:(gi + 1) * cg] = _dot(hg, c_ref[...]).astype(BF16)
        o_ref[:, d + gi * cg:d + (gi + 1) * cg] = _dot(hg, s_ref[...]).astype(BF16)


def _fnet_chan(st, x, g, mods, cc, sc, tm=TOKEN_TILE):
    return pl.pallas_call(
        _fnet_chan_kernel,
        grid=st.grid(tm),
        in_specs=[st.tok_spec(tm), _const_spec((1, st.d)), st.mod_spec(),
                  _const_spec(cc.shape), _const_spec(sc.shape)],
        out_specs=st.tok_spec(tm, 2 * st.d),
        out_shape=st.shape(2 * st.d, BF16),
        compiler_params=_cparams(_PAR2),
        name="fnet_chan",
    )(x, g.reshape(1, st.d), mods, cc, sc)


def _fnet_seq_kernel(scale, x_ref, xcs_ref, cl_ref, sl_ref, m_ref, wo_ref, o_ref, acc_ref):
    kstep = pl.program_id(2)
    d = x_ref.shape[-1]

    @pl.when(kstep == 0)
    def _():
        acc_ref[...] = jnp.zeros_like(acc_ref)

    acc_ref[...] += _dot(cl_ref[...], xcs_ref[:, :d]) - _dot(sl_ref[...], xcs_ref[:, d:])

    @pl.when(kstep == pl.num_programs(2) - 1)
    def _():
        f = (acc_ref[...] * scale).astype(BF16)
        o_ref[...] = x_ref[...] + m_ref[...][2:3] * _dot(f, wo_ref[...])


def _fnet_seq(st, x, xcs, mods, wo, tl=DFT_OUT_TILE, tk=DFT_K_TILE):
    d, l = st.d, st.l
    tl = min(tl, l)
    tk = min(tk, l)
    cl, sl = _dft_mats(l)
    cl = cl.astype(BF16)
    sl = sl.astype(BF16)
    scale = 1.0 / float(l * (d // FOURIER_GROUPS)) ** 0.5
    return pl.pallas_call(
        functools.partial(_fnet_seq_kernel, scale),
        grid=(st.b, l // tl, l // tk),
        in_specs=[
            pl.BlockSpec((None, tl, d), lambda b, i, k: (b, i, 0)),
            pl.BlockSpec((None, tk, 2 * d), lambda b, i, k: (b, k, 0)),
            pl.BlockSpec((tl, tk), lambda b, i, k: (i, k)),
            pl.BlockSpec((tl, tk), lambda b, i, k: (i, k)),
            st.mod_spec(),
            _const_spec(wo.shape),
        ],
        out_specs=pl.BlockSpec((None, tl, d), lambda b, i, k: (b, i, 0)),
        out_shape=st.shape(),
        scratch_shapes=[pltpu.VMEM((tl, d), F32)],
        compiler_params=_cparams(("parallel", "parallel", "arbitrary")),
        name="fnet_seq",
    )(x, xcs, cl, sl, mods, wo)


def _fft1_kernel(x_ref, g_ref, m_ref, cc_ref, sc_ref, w1_ref, tc_ref, ts_ref, o_ref):
    n, j_n, d = x_ref.shape
    cg = d // FOURIER_GROUPS
    m = m_ref[...]
    xs = jnp.concatenate([x_ref[:, j, :] for j in range(j_n)], axis=0)
    h = _norm_mod(xs, g_ref[...], m[0:1], m[1:2]).astype(BF16)
    xc = []
    xsn = []
    for gi in range(FOURIER_GROUPS):
        hg = h[:, gi * cg:(gi + 1) * cg]
        xc.append(_dot(hg, cc_ref[...]).astype(BF16))
        xsn.append(_dot(hg, sc_ref[...]).astype(BF16))
    xcs = jnp.concatenate(xc + xsn, axis=1)
    w1 = w1_ref[...]
    for j in range(j_n):
        p = _dot(w1, xcs[j * n:(j + 1) * n])
        a_re = p[:n, :d] - p[n:, d:]
        a_im = -(p[:n, d:] + p[n:, :d])
        tc = jnp.concatenate([tc_ref[j]] * (d // LANES), axis=1)
        ts = jnp.concatenate([ts_ref[j]] * (d // LANES), axis=1)
        o_ref[:, j, :d] = (a_re * tc + a_im * ts).astype(BF16)
        o_ref[:, j, d:] = (a_im * tc - a_re * ts).astype(BF16)


def _fft2_kernel(scale, final, x_ref, b_ref, m_ref, w2_ref, wo_ref, *rest):
    mlp, final_ref, o_ref = rest[:3], (rest[3] if final else None), rest[-1]
    n, j_n, d = x_ref.shape
    w2 = w2_ref[...]
    m = m_ref[...]
    f = []
    for j in range(j_n):
        bj = b_ref[j]
        f.append(_dot(w2, jnp.concatenate([bj[:, :d], bj[:, d:]], axis=0)))
    fa = (jnp.concatenate(f, axis=0) * scale).astype(BF16)
    xs = jnp.concatenate([x_ref[:, j, :] for j in range(j_n)], axis=0)
    y = _mlp_rows(xs + m[2:3] * _dot(fa, wo_ref[...]), m, mlp, final_ref)
    for j in range(j_n):
        o_ref[:, j, :] = y[j * n:(j + 1) * n]


def _fnet_fft(st, x, g, mods, cc, sc, wo, mlp_g, mlp_w1, mlp_w2, final_g=None):
    nb, l, d = st.b, st.l, st.d
    n, jb = FFT_N, FFT_J
    assert l == n * n
    c1, s1 = _dft_mats(n)
    w1 = jnp.concatenate([c1, s1], axis=0).astype(BF16)
    w2 = jnp.concatenate([c1, s1], axis=1).astype(BF16)
    idx = jnp.arange(n, dtype=jnp.int32)
    ang = (idx[:, None] * idx[None, :]).astype(F32) * (2.0 * jnp.pi / l)
    tc = jnp.broadcast_to(jnp.cos(ang)[:, :, None], (n, n, LANES))
    ts = jnp.broadcast_to(jnp.sin(ang)[:, :, None], (n, n, LANES))
    scale = 1.0 / float(l * (d // FOURIER_GROUPS)) ** 0.5
    x4 = x.reshape(nb, n, n, d)
    blk = pl.BlockSpec((None, n, jb, d), lambda b, i: (b, 0, i, 0))
    tw = pl.BlockSpec((jb, n, LANES), lambda b, i: (i, 0, 0))
    bmat = pl.pallas_call(
        _fft1_kernel,
        grid=(nb, n // jb),
        in_specs=[blk, _const_spec((1, d)), st.mod_spec(), _const_spec(cc.shape), _const_spec(sc.shape),
                  _const_spec(w1.shape), tw, tw],
        out_specs=pl.BlockSpec((None, n, jb, 2 * d), lambda b, i: (b, 0, i, 0)),
        out_shape=jax.ShapeDtypeStruct((nb, n, n, 2 * d), BF16),
        compiler_params=_cparams(_PAR2),
        name="fnet_fft1",
    )(x4, g.reshape(1, d), mods, cc, sc, w1, tc, ts)
    ops, specs = _mlp_consts(st, mlp_g, mlp_w1, mlp_w2, final_g)
    out = pl.pallas_call(
        functools.partial(_fft2_kernel, scale, final_g is not None),
        grid=(nb, n // jb),
        in_specs=[blk, pl.BlockSpec((None, jb, n, 2 * d), lambda b, i: (b, i, 0, 0)),
                  st.mod_spec(), _const_spec(w2.shape), _const_spec(wo.shape)] + specs,
        out_specs=blk,
        out_shape=jax.ShapeDtypeStruct((nb, n, n, d), F32),
        compiler_params=_cparams(_PAR2),
        name="fnet_fft2_mlp",
    )(x4, bmat, mods, w2, wo, *ops)
    return out.reshape(nb, l, d)


def _blockdiag2(w):
    z = jnp.zeros_like(w[0])
    return jnp.concatenate([jnp.concatenate([w[0], z], axis=1),
                            jnp.concatenate([z, w[1]], axis=1)], axis=0)


def kernel(x, c, ctx, c_ctx, norm1_g, norm2_g, mod_w, mod_b, mlp_w1, mlp_w2, rk_mu, rk_wr, rk_wk, rk_wv, rk_wo, rk_w0, rk_w1, rk_w2, rk_a0, rk_a1, rk_a2, rk_v0, rk_v1, rk_v2, rk_g1, rk_g2, rk_kk, rk_ka, rk_rk, rk_lnx_w, rk_lnx_b, ft_wo, final_g):
    nb, seq, d = x.shape
    ctx_len = ctx.shape[1]
    depth = mod_w.shape[0]
    assert nb + 1 <= MOD_ROWS and d % LANES == 0 and seq == FFT_N * FFT_N
    lat = _Stream(nb, seq, d)
    cst = _Stream(nb, ctx_len, d, ctx_row=nb)
    row = lambda t: t.reshape(1, -1)

    cvec = jnp.zeros((MOD_ROWS, d), F32).at[:nb].set(c).at[nb].set(c_ctx)
    mods_all = _mod_table(cvec, mod_w, mod_b).reshape(depth, MOD_ROWS, N_MOD, d)

    heads = jnp.arange(d) // HEAD_DIM
    s1 = (heads[:, None] == jnp.arange(LANES)[None, :]).astype(BF16)
    s2 = s1.T
    cg = d // FOURIER_GROUPS
    cc, sc = _dft_mats(cg)
    cc = cc.astype(BF16)
    sc = sc.astype(BF16)
    zero_state = jnp.zeros((2, nb, d // GROUP, HEAD_DIM, GROUP), F32)

    last_a = ((depth - 1) // 2) * 2
    vf_lat = vf_ctx = None
    for i in range(depth):
        idx = i // 2
        mods = mods_all[i]
        ctx_out = i < last_a
        w1 = mlp_w1[i].astype(BF16)
        w2 = mlp_w2[i].astype(BF16)
        fin = final_g if i == depth - 1 else None
        if i % 2 == 0:
            p = {
                'mu': rk_mu[idx], 'wr': rk_wr[idx].astype(BF16), 'wk': rk_wk[idx].astype(BF16),
                'wv': rk_wv[idx].astype(BF16), 'wo': rk_wo[idx].astype(BF16),
                'w1': jnp.concatenate([rk_w1[idx, 0], rk_w1[idx, 1]], axis=1).astype(BF16),
                'w2': _blockdiag2(rk_w2[idx]).astype(BF16), 'w0': row(rk_w0[idx]),
                'a1': jnp.concatenate([rk_a1[idx, 0], rk_a1[idx, 1]], axis=1).astype(BF16),
                'a2': _blockdiag2(rk_a2[idx]).astype(BF16), 'a0': row(rk_a0[idx]),
                'g1': rk_g1[idx].astype(BF16), 'g2': rk_g2[idx].astype(BF16),
                'kk': row(rk_kk[idx]), 'ka': row(rk_ka[idx]), 'rk': row(rk_rk[idx]),
                'lnx_w': row(rk_lnx_w[idx]), 'lnx_b': row(rk_lnx_b[idx]), 's1': s1, 's2': s2,
            }
            if idx > 0:
                p['v0'] = row(rk_v0[idx - 1])
                p['v1'] = rk_v1[idx - 1].astype(BF16)
                p['v2'] = rk_v2[idx - 1].astype(BF16)
            v_c, g_c, bonus_c, a_c, r_c, lw_c, k_c, b_c = _rwkv_proj(
                cst, ctx, norm1_g[i], mods, vf_ctx if idx > 0 else None, p, 0, tm=ctx_len)
            y_c, s_ctx = _wkv(cst, a_c, r_c, v_c, lw_c, k_c, b_c, zero_state)
            v_l, g_l, bonus_l, a_l, r_l, lw_l, k_l, b_l = _rwkv_proj(
                lat, x, norm1_g[i], mods, vf_lat if idx > 0 else None, p, GRID_W)
            y_l, _ = _wkv(lat, a_l, r_l, v_l, lw_l, k_l, b_l, s_ctx)
            if idx == 0:
                vf_ctx, vf_lat = v_c, v_l
            x = _rwkv_out(lat, x, y_l, bonus_l, g_l, mods, p, norm2_g[i], w1, w2, fin)
            if ctx_out:
                ctx = _rwkv_out(cst, ctx, y_c, bonus_c, g_c, mods, p, norm2_g[i], w1, w2)
        else:
            wo = ft_wo[idx].astype(BF16)
            x = _fnet_fft(lat, x, norm1_g[i], mods, cc, sc, wo, norm2_g[i], w1, w2, fin)
            if ctx_out:
                ctx = _fnet_seq(cst, ctx, _fnet_chan(cst, ctx, norm1_g[i], mods, cc, sc), mods, wo)
                ctx = _mlp(cst, ctx, mods, norm2_g[i], w1, w2)
    return x
```
